```python
import jax, jax.numpy as jnp
from jax import lax
import numpy as np

D_MODEL = 4096
BATCH = 2
SEQ = 8192
DEPTH = 4

GRID_W = 64
CTX_LEN = 256
MIX_WIDTH = 2 * D_MODEL
D_SSD = MIX_WIDTH // 2
SSD_HEAD_DIM = 64
SSD_HEADS = D_SSD // SSD_HEAD_DIM
SSD_GROUPS = 8
SSD_HPG = SSD_HEADS // SSD_GROUPS
SSD_STATE = 128
SSD_CONV_CH = D_SSD + 2 * SSD_GROUPS * SSD_STATE
D_RET = MIX_WIDTH - D_SSD
RET_HEADS = 16
RET_V_DIM = D_RET // RET_HEADS
RET_QK_DIM = 128
D_RNN = MIX_WIDTH
RNN_BLOCK = 256
RNN_BLOCKS = D_RNN // RNN_BLOCK
LRU_C = 8.0
CONV_W = 4
CHUNK = 128
ROPE_BASE = 10000.0
EPS = 1e-6
EVEN_SPLITS = (D_SSD, D_SSD, SSD_GROUPS * SSD_STATE, SSD_GROUPS * SSD_STATE, 2 * SSD_HEADS,
               RET_HEADS * RET_QK_DIM, RET_HEADS * RET_QK_DIM, D_RET, D_RET)
EVEN_IN = sum(EVEN_SPLITS)
EVEN_IDX = [int(s) for s in np.cumsum(EVEN_SPLITS)[:-1]]

kernel_name = "hybrid_ssd_retention_rglru_diffusion_trunk"

F32 = jnp.float32


def rms_norm(x, g):
    xf = x.astype(F32)
    y = xf * lax.rsqrt(jnp.mean(xf * xf, axis=-1, keepdims=True) + EPS)
    return (y * g.astype(F32)).astype(x.dtype)


def centred_dwconv(x, w, b):
    l = x.shape[1]
    left = CONV_W // 2
    xp = jnp.pad(x, ((0, 0), (left, CONV_W - 1 - left), (0, 0)))
    y = b
    for k in range(CONV_W):
        y = y + w[k] * xp[:, k:k + l]
    return y


def rope(x, pos):
    half = x.shape[-1] // 2
    freq = ROPE_BASE ** (-jnp.arange(half, dtype=F32) / half)
    ang = pos.astype(F32)[:, None] * freq
    cos, sin = jnp.cos(ang)[None, :, None], jnp.sin(ang)[None, :, None]
    xf = x.astype(F32)
    x1, x2 = xf[..., :half], xf[..., half:]
    return jnp.concatenate([x1 * cos - x2 * sin, x1 * sin + x2 * cos], axis=-1).astype(x.dtype)


def rope_2d(x, row, col):
    d = x.shape[-1] // 2
    return jnp.concatenate([rope(x[..., :d], row), rope(x[..., d:], col)], axis=-1)


def chunk_scan(q, k, v, log_a, h0=None):
    b, l, g, n = q.shape
    r, p = v.shape[3], v.shape[4]
    nc = l // CHUNK
    qc = q.astype(F32).reshape(b, nc, CHUNK, g, n)
    kc = k.astype(F32).reshape(b, nc, CHUNK, g, n)
    vc = v.astype(F32).reshape(b, nc, CHUNK, g, r, p)
    acs = jnp.cumsum(log_a.astype(F32).reshape(b, nc, CHUNK, g, r), axis=2)
    tri = jnp.tril(jnp.ones((CHUNK, CHUNK), bool))[None, None, :, :, None, None]
    seg = acs[:, :, :, None] - acs[:, :, None, :]
    decay = jnp.exp(jnp.where(tri, seg, -jnp.inf))
    scores = jnp.einsum('bcign,bcjgn->bcijg', qc, kc)
    y = jnp.einsum('bcijg,bcijgr,bcjgrp->bcigrp', scores, decay, vc)
    to_end = jnp.exp(acs[:, :, -1:] - acs)
    states = jnp.einsum('bcjgn,bcjgr,bcjgrp->bcgrpn', kc, to_end, vc)
    chunk_decay = jnp.exp(acs[:, :, -1])
    if h0 is None:
        h0 = jnp.zeros((b, g, r, p, n), F32)

    def step(h, inp):
        s, dcy = inp
        return dcy[..., None, None] * h + s, h

    h_last, h_prev = lax.scan(step, h0.astype(F32),
                              (jnp.moveaxis(states, 1, 0), jnp.moveaxis(chunk_decay, 1, 0)))
    h_prev = jnp.moveaxis(h_prev, 0, 1)
    y = y + jnp.einsum('bcign,bcgrpn,bcigr->bcigrp', qc, h_prev, jnp.exp(acs))
    return y.reshape(b, l, g, r, p), h_last


def rglru_scan(log_a, u, h0=None):
    def combine(e1, e2):
        a1, b1 = e1
        a2, b2 = e2
        return a1 * a2, a2 * b1 + b2

    a_cum, h = lax.associative_scan(combine, (jnp.exp(log_a), u), axis=1)
    if h0 is not None:
        h = h + a_cum * h0[:, None]
    return h, h[:, -1]


def bidir(scan, fwd_ctx, fwd_lat, bwd_ctx, bwd_lat):
    flip = lambda t: jnp.flip(t, axis=1)
    yc_f, hc_f = scan(*fwd_ctx)
    yl_f, _ = scan(*fwd_lat, hc_f)
    yc_b, hc_b = scan(*[flip(t) for t in bwd_ctx])
    yl_b, _ = scan(*[flip(t) for t in bwd_lat], hc_b)
    return yc_f + flip(yc_b), yl_f + flip(yl_b)


def retention_log_decay():
    return jnp.log1p(-jnp.power(2.0, -5.0 - jnp.arange(RET_HEADS, dtype=F32)))


def even_prep(pr, conv_w, conv_b, dt_bias, a_log, pos):
    b, l = pr.shape[:2]
    z, xs, bm, cm, dt, q, k, v, g = jnp.split(pr, EVEN_IDX, axis=-1)
    xbc = jax.nn.silu(centred_dwconv(jnp.concatenate([xs, bm, cm], axis=-1), conv_w, conv_b))
    xs, bm, cm = jnp.split(xbc, [D_SSD, D_SSD + SSD_GROUPS * SSD_STATE], axis=-1)
    xs = xs.reshape(b, l, SSD_GROUPS, SSD_HPG, SSD_HEAD_DIM)
    bm = bm.reshape(b, l, SSD_GROUPS, SSD_STATE)
    cm = cm.reshape(b, l, SSD_GROUPS, SSD_STATE)
    dt = jax.nn.softplus(dt.astype(F32).reshape(b, l, 2, SSD_GROUPS, SSD_HPG)
                         + dt_bias.astype(F32).reshape(2, SSD_GROUPS, SSD_HPG))
    a = -jnp.exp(a_log.astype(F32)).reshape(2, SSD_GROUPS, SSD_HPG)
    xs_f = xs.astype(F32)
    ssd_f = (cm, bm, xs_f * dt[:, :, 0, ..., None], dt[:, :, 0] * a[0])
    ssd_b = (cm, bm, xs_f * dt[:, :, 1, ..., None], dt[:, :, 1] * a[1])
    q = q.reshape(b, l, RET_HEADS, RET_QK_DIM)
    k = k.reshape(b, l, RET_HEADS, RET_QK_DIM)
    if pos is not None:
        q, k = rope_2d(q, *pos), rope_2d(k, *pos)
    k = k * (RET_QK_DIM ** -0.5)
    v = v.reshape(b, l, RET_HEADS, 1, RET_V_DIM)
    lg = retention_log_decay()
    ret_f = (q, k, v, jnp.broadcast_to(lg[:, None], (b, l, RET_HEADS, 1)))
    ret_b = (q, k, v, jnp.broadcast_to(lg[::-1][:, None], (b, l, RET_HEADS, 1)))
    return ssd_f, ssd_b, ret_f, ret_b, xs, z, g


def even_out(ys, yr, xs, z, g, d_skip, ssd_norm, ret_norm, w_out):
    b, l = z.shape[:2]
    ys = ys + d_skip.astype(F32).reshape(SSD_GROUPS, SSD_HPG)[..., None] * xs.astype(F32)
    ys = rms_norm(ys.reshape(b, l, D_SSD).astype(z.dtype) * jax.nn.silu(z), ssd_norm)
    yr = yr[:, :, :, 0]
    mu = jnp.mean(yr, axis=-1, keepdims=True)
    var = jnp.mean(jnp.square(yr - mu), axis=-1, keepdims=True)
    yr = ((yr - mu) * lax.rsqrt(var + EPS)).reshape(b, l, D_RET) * ret_norm.astype(F32)
    yr = yr.astype(g.dtype) * jax.nn.silu(g)
    return jnp.concatenate([ys, yr], axis=-1) @ w_out


def even_mixer(pc, pl, pos, conv_w, conv_b, dt_bias, a_log, d_skip, ssd_norm, ret_norm, w_out):
    sf_c, sb_c, rf_c, rb_c, xs_c, z_c, g_c = even_prep(pc, conv_w, conv_b, dt_bias, a_log, None)
    sf_l, sb_l, rf_l, rb_l, xs_l, z_l, g_l = even_prep(pl, conv_w, conv_b, dt_bias, a_log, pos)
    ys_c, ys_l = bidir(chunk_scan, sf_c, sf_l, sb_c, sb_l)
    yr_c, yr_l = bidir(chunk_scan, rf_c, rf_l, rb_c, rb_l)
    out_c = even_out(ys_c, yr_c, xs_c, z_c, g_c, d_skip, ssd_norm, ret_norm, w_out)
    out_l = even_out(ys_l, yr_l, xs_l, z_l, g_l, d_skip, ssd_norm, ret_norm, w_out)
    return out_c, out_l


def odd_prep(pr, conv_w, conv_b, gate_w, gate_b, lam):
    b, l = pr.shape[:2]
    gate, xr = jnp.split(pr, 2, axis=-1)
    xf = centred_dwconv(xr, conv_w, conv_b).astype(F32)
    xb = xf.reshape(b, l, RNN_BLOCKS, RNN_BLOCK)
    dirs = []
    for d in range(2):
        r = jax.nn.sigmoid(jnp.einsum('blkc,kcf->blkf', xb, gate_w[d, 0].astype(F32)).reshape(b, l, D_RNN)
                           + gate_b[d, 0].astype(F32))
        i = jax.nn.sigmoid(jnp.einsum('blkc,kcf->blkf', xb, gate_w[d, 1].astype(F32)).reshape(b, l, D_RNN)
                           + gate_b[d, 1].astype(F32))
        log_a = -LRU_C * jax.nn.softplus(-lam[d].astype(F32)) * r
        u = jnp.sqrt(-jnp.expm1(2.0 * log_a)) * (i * xf)
        dirs.append((log_a, u))
    return dirs[0], dirs[1], gate


def odd_mixer(pc, pl, conv_w, conv_b, gate_w, gate_b, lam, w_out):
    f_c, b_c, gate_c = odd_prep(pc, conv_w, conv_b, gate_w, gate_b, lam)
    f_l, b_l, gate_l = odd_prep(pl, conv_w, conv_b, gate_w, gate_b, lam)
    y_c, y_l = bidir(rglru_scan, f_c, f_l, b_c, b_l)
    out_c = (y_c.astype(gate_c.dtype) * jax.nn.silu(gate_c)) @ w_out
    out_l = (y_l.astype(gate_l.dtype) * jax.nn.silu(gate_l)) @ w_out
    return out_c, out_l


def setup_inputs(seed: int = 0) -> dict:
    key = jax.random.key(seed)
    ks = iter(jax.random.split(key, 32))
    n_even, n_odd = (DEPTH + 1) // 2, DEPTH // 2
    nrm = lambda shape, scale: jax.random.normal(next(ks), shape, F32) * scale
    uni = lambda shape, lo, hi: jax.random.uniform(next(ks), shape, F32, lo, hi)
    x = nrm((BATCH, SEQ, D_MODEL), 1.0)
    c = nrm((BATCH, D_MODEL), 1.0)
    ctx = nrm((BATCH, CTX_LEN, D_MODEL), 1.0)
    c_ctx = nrm((D_MODEL,), 1.0)
    ada_w = nrm((DEPTH, D_MODEL, 3 * D_MODEL), D_MODEL ** -0.5)
    ada_b = nrm((DEPTH, 3 * D_MODEL), 0.02)
    norm_pre = 1.0 + nrm((DEPTH, D_MODEL), 0.02)
    norm_post = 1.0 + nrm((DEPTH, D_MODEL), 0.02)
    e_w_in = nrm((n_even, D_MODEL, EVEN_IN), D_MODEL ** -0.5)
    e_conv_w = nrm((n_even, CONV_W, SSD_CONV_CH), CONV_W ** -0.5)
    e_conv_b = nrm((n_even, SSD_CONV_CH), 0.02)
    dt0 = jnp.exp(uni((n_even, 2, SSD_HEADS), float(np.log(1e-3)), float(np.log(1e-1))))
    e_dt_bias = dt0 + jnp.log(-jnp.expm1(-dt0))
    e_a_log = jnp.log(uni((n_even, 2, SSD_HEADS), 1.0, 16.0))
    e_d_skip = 1.0 + nrm((n_even, SSD_HEADS), 0.1)
    e_ssd_norm = 1.0 + nrm((n_even, D_SSD), 0.02)
    e_ret_norm = 1.0 + nrm((n_even, D_RET), 0.02)
    e_w_out = nrm((n_even, MIX_WIDTH, D_MODEL), MIX_WIDTH ** -0.5)
    o_w_in = nrm((n_odd, D_MODEL, 2 * D_RNN), D_MODEL ** -0.5)
    o_conv_w = nrm((n_odd, CONV_W, D_RNN), CONV_W ** -0.5)
    o_conv_b = nrm((n_odd, D_RNN), 0.02)
    o_gate_w = nrm((n_odd, 2, 2, RNN_BLOCKS, RNN_BLOCK, RNN_BLOCK), RNN_BLOCK ** -0.5)
    o_gate_b = nrm((n_odd, 2, 2, D_RNN), 0.02)
    sig = uni((n_odd, 2, D_RNN), 0.9, 0.999) ** (1.0 / LRU_C)
    o_lambda = jnp.log(sig) - jnp.log1p(-sig)
    o_w_out = nrm((n_odd, D_RNN, D_MODEL), D_RNN ** -0.5)
    return {"x": x, "c": c, "ctx": ctx, "c_ctx": c_ctx, "ada_w": ada_w, "ada_b": ada_b,
            "norm_pre": norm_pre, "norm_post": norm_post, "e_w_in": e_w_in, "e_conv_w": e_conv_w,
            "e_conv_b": e_conv_b, "e_dt_bias": e_dt_bias, "e_a_log": e_a_log, "e_d_skip": e_d_skip,
            "e_ssd_norm": e_ssd_norm, "e_ret_norm": e_ret_norm, "e_w_out": e_w_out, "o_w_in": o_w_in,
            "o_conv_w": o_conv_w, "o_conv_b": o_conv_b, "o_gate_w": o_gate_w, "o_gate_b": o_gate_b,
            "o_lambda": o_lambda, "o_w_out": o_w_out}


def reference(x, c, ctx, c_ctx, ada_w, ada_b, norm_pre, norm_post, e_w_in, e_conv_w, e_conv_b,
              e_dt_bias, e_a_log, e_d_skip, e_ssd_norm, e_ret_norm, e_w_out, o_w_in, o_conv_w,
              o_conv_b, o_gate_w, o_gate_b, o_lambda, o_w_out):
    n_lat = x.shape[1]
    rows = n_lat // GRID_W
    pos = (jnp.repeat(jnp.arange(rows), GRID_W), jnp.tile(jnp.arange(GRID_W), rows))
    for i in range(DEPTH):
        j = i // 2
        mod_l = (jax.nn.silu(c) @ ada_w[i] + ada_b[i])[:, None, :]
        mod_c = jax.nn.silu(c_ctx) @ ada_w[i] + ada_b[i]
        sh_l, sc_l, gt_l = jnp.split(mod_l, 3, axis=-1)
        sh_c, sc_c, gt_c = jnp.split(mod_c, 3, axis=-1)
        hl = rms_norm(x, norm_pre[i]) * (1.0 + sc_l) + sh_l
        hc = rms_norm(ctx, norm_pre[i]) * (1.0 + sc_c) + sh_c
        if i % 2 == 0:
            out_c, out_l = even_mixer(hc @ e_w_in[j], hl @ e_w_in[j], pos, e_conv_w[j], e_conv_b[j],
                                      e_dt_bias[j], e_a_log[j], e_d_skip[j], e_ssd_norm[j],
                                      e_ret_norm[j], e_w_out[j])
        else:
            out_c, out_l = odd_mixer(hc @ o_w_in[j], hl @ o_w_in[j], o_conv_w[j], o_conv_b[j],
                                     o_gate_w[j], o_gate_b[j], o_lambda[j], o_w_out[j])
        x = x + gt_l * rms_norm(out_l, norm_post[i])
        ctx = ctx + gt_c * rms_norm(out_c, norm_post[i])
    return x
```

```python
import functools
import math

import numpy as np
import jax
import jax.numpy as jnp
from jax import lax
from jax.experimental import pallas as pl
from jax.experimental.pallas import tpu as pltpu

F32 = jnp.float32
BF16 = jnp.bfloat16

GRID_W = 64
CTX_LEN = 256
SSD_HEAD_DIM = 64
SSD_GROUPS = 8
SSD_STATE = 128
RET_HEADS = 16
RET_QK_DIM = 128
RNN_BLOCK = 256
LRU_C = 8.0
CONV_W = 4
CHUNK = 128
ROPE_BASE = 10000.0
EPS = 1e-6

ROW_TILE = 256
VMEM_LIMIT = 56 * 1024 * 1024
NEG_BIG = -1e30


def _cparams(*sem):
    return pltpu.CompilerParams(dimension_semantics=sem, vmem_limit_bytes=VMEM_LIMIT)


def _silu(x):
    return x * jax.nn.sigmoid(x)


def _softplus(x):
    return jnp.maximum(x, 0.0) + jnp.log1p(jnp.exp(-jnp.abs(x)))


def _mod_kernel(ct_ref, w_ref, b_ref, o_ref, *, n_vec):
    w = w_ref[0]
    o_ref[0] = jnp.zeros(o_ref.shape[1:], F32)
    for m in range(n_vec):
        col = _silu(ct_ref[:, m:m + 1])
        o_ref[0, m:m + 1, :] = jnp.sum(w * col, axis=0, keepdims=True) + b_ref[0]


def _modulation(cvecs, ada_w, ada_b):
    depth, d, n3 = ada_w.shape
    n_vec = cvecs.shape[0]
    ct = jnp.zeros((d, 8), F32).at[:, :n_vec].set(cvecs.T)
    tn = 512
    return pl.pallas_call(
        functools.partial(_mod_kernel, n_vec=n_vec),
        grid=(depth, n3 // tn),
        in_specs=[pl.BlockSpec((d, 8), lambda i, j: (0, 0)),
                  pl.BlockSpec((1, d, tn), lambda i, j: (i, 0, j)),
                  pl.BlockSpec((1, 1, tn), lambda i, j: (i, 0, j))],
        out_specs=pl.BlockSpec((1, 8, tn), lambda i, j: (i, 0, j)),
        out_shape=jax.ShapeDtypeStruct((depth, 8, n3), F32),
        name="modulation",
        compiler_params=_cparams("parallel", "parallel"),
    )(ct, ada_w, ada_b.reshape(depth, 1, n3))


def _prenorm_kernel(x_ref, g_ref, mod_ref, o_ref):
    x = x_ref[...]
    y = x * lax.rsqrt(jnp.mean(x * x, axis=-1, keepdims=True) + EPS) * g_ref[...]
    o_ref[...] = (y * (1.0 + mod_ref[0, 1:2, :]) + mod_ref[0, 0:1, :]).astype(o_ref.dtype)


def _prenorm(x2, g, modblk):
    m, d = x2.shape
    r = ROW_TILE
    return pl.pallas_call(
        _prenorm_kernel,
        grid=(m // r,),
        in_specs=[pl.BlockSpec((r, d), lambda i: (i, 0)),
                  pl.BlockSpec((1, d), lambda i: (0, 0)),
                  pl.BlockSpec((1, 3, d), lambda i: (i, 0, 0))],
        out_specs=pl.BlockSpec((r, d), lambda i: (i, 0)),
        out_shape=jax.ShapeDtypeStruct((m, d), BF16),
        name="prenorm",
        compiler_params=_cparams("parallel"),
    )(x2, g.reshape(1, d), modblk)


def _mm_kernel(a_ref, b_ref, o_ref):
    o_ref[...] = jnp.dot(a_ref[...], b_ref[...], preferred_element_type=F32).astype(o_ref.dtype)


def _matmul(a, b, tm, tn, out_dtype=F32):
    m, k = a.shape
    n = b.shape[1]
    return pl.pallas_call(
        _mm_kernel,
        grid=(m // tm, n // tn),
        in_specs=[pl.BlockSpec((tm, k), lambda i, j: (i, 0)),
                  pl.BlockSpec((k, tn), lambda i, j: (0, j))],
        out_specs=pl.BlockSpec((tm, tn), lambda i, j: (i, j)),
        out_shape=jax.ShapeDtypeStruct((m, n), out_dtype),
        name="in_proj",
        compiler_params=_cparams("parallel", "arbitrary"),
    )(a, b)


def _outproj_kernel(a_ref, w_ref, x_ref, mod_ref, g_ref, o_ref, *, nk):
    k = pl.program_id(1)
    part = jnp.dot(a_ref[...], w_ref[...], preferred_element_type=F32)

    @pl.when(k == 0)
    def _():
        o_ref[...] = part

    @pl.when(k > 0)
    def _():
        o_ref[...] += part

    @pl.when(k == nk - 1)
    def _():
        for s in range(o_ref.shape[0] // ROW_TILE):
            rows = slice(s * ROW_TILE, (s + 1) * ROW_TILE)
            acc = o_ref[rows, :]
            y = acc * lax.rsqrt(jnp.mean(acc * acc, axis=-1, keepdims=True) + EPS) * g_ref[...]
            o_ref[rows, :] = x_ref[rows, :] + mod_ref[s, 2:3, :] * y


def _outproj(a, w, x2, modblk, g, tm, tk):
    m, kdim = a.shape
    d = w.shape[1]
    nk = kdim // tk
    nsub = tm // ROW_TILE
    return pl.pallas_call(
        functools.partial(_outproj_kernel, nk=nk),
        grid=(m // tm, nk),
        in_specs=[pl.BlockSpec((tm, tk), lambda i, k: (i, k)),
                  pl.BlockSpec((tk, d), lambda i, k: (k, 0)),
                  pl.BlockSpec((tm, d), lambda i, k: (i, 0)),
                  pl.BlockSpec((nsub, 3, d), lambda i, k: (i, 0, 0)),
                  pl.BlockSpec((1, d), lambda i, k: (0, 0))],
        out_specs=pl.BlockSpec((tm, d), lambda i, k: (i, 0)),
        out_shape=jax.ShapeDtypeStruct((m, d), F32),
        name="out_proj",
        compiler_params=_cparams("parallel", "arbitrary"),
    )(a, w, x2, modblk, g.reshape(1, d))


def _conv_taps(x, prev8, next8, w_ref, b_ref, tile, n_tiles):
    r = x.shape[0]
    ctx_tiles = CTX_LEN // r
    prev_ok = jnp.logical_and(tile != 0, tile != ctx_tiles)
    next_ok = jnp.logical_and(tile != ctx_tiles - 1, tile != n_tiles - 1)
    pm = jnp.where(prev_ok, prev8, 0.0)
    nm = jnp.where(next_ok, next8, 0.0)
    row = lax.broadcasted_iota(jnp.int32, x.shape, 0)
    xm1 = jnp.where(row == 0, pm[7:8], pltpu.roll(x, 1, 0))
    xm2 = jnp.where(row == 0, pm[6:7], jnp.where(row == 1, pm[7:8], pltpu.roll(x, 2, 0)))
    xp1 = jnp.where(row == r - 1, nm[0:1], pltpu.roll(x, r - 1, 0))
    return (b_ref[...] + w_ref[0:1, :] * xm2 + w_ref[1:2, :] * xm1
            + w_ref[2:3, :] * x + w_ref[3:4, :] * xp1)


def _halo_specs(r, cw, col_map, n_tiles):
    rb = r // 8
    last = n_tiles * rb - 1
    return [
        pl.BlockSpec((1, r, cw), lambda b, t, j: (b, t, col_map(j))),
        pl.BlockSpec((1, 8, cw), lambda b, t, j: (b, jnp.maximum(t * rb - 1, 0), col_map(j))),
        pl.BlockSpec((1, 8, cw), lambda b, t, j: (b, jnp.minimum((t + 1) * rb, last), col_map(j))),
    ]


def _conv_silu_kernel(x_ref, p_ref, n_ref, w_ref, b_ref, o_ref, *, n_tiles):
    y = _conv_taps(x_ref[0], p_ref[0], n_ref[0], w_ref, b_ref, pl.program_id(1), n_tiles)
    o_ref[0] = _silu(y).astype(o_ref.dtype)


def _conv_silu(pr, conv_w, conv_b, col_map, n_cols, cw):
    bsz, t, _ = pr.shape
    r = ROW_TILE
    n_tiles = t // r
    return pl.pallas_call(
        functools.partial(_conv_silu_kernel, n_tiles=n_tiles),
        grid=(bsz, n_tiles, n_cols // cw),
        in_specs=_halo_specs(r, cw, col_map, n_tiles) + [
            pl.BlockSpec((CONV_W, cw), lambda b, t, j: (0, j)),
            pl.BlockSpec((1, cw), lambda b, t, j: (0, j))],
        out_specs=pl.BlockSpec((1, r, cw), lambda b, t, j: (b, t, j)),
        out_shape=jax.ShapeDtypeStruct((bsz, t, n_cols), BF16),
        name="conv_silu",
        compiler_params=_cparams("parallel", "parallel", "parallel"),
    )(pr, pr, pr, conv_w, conv_b.reshape(1, n_cols))


def _rope_kernel(x_ref, cos_ref, sa_ref, sb_ref, o_ref, *, scale):
    cos, sa, sb = cos_ref[...], sa_ref[...], sb_ref[...]
    q4 = RET_QK_DIM // 4
    for h in range(x_ref.shape[2] // RET_QK_DIM):
        cols = slice(h * RET_QK_DIM, (h + 1) * RET_QK_DIM)
        x = x_ref[0, :, cols]
        y = x * cos + pltpu.roll(x, RET_QK_DIM - q4, 1) * sa + pltpu.roll(x, q4, 1) * sb
        o_ref[0, :, cols] = (y * scale).astype(o_ref.dtype)


def _rope(pr, col_blk, width, tabs, scale):
    bsz, t, _ = pr.shape
    r = ROW_TILE
    tab_spec = pl.BlockSpec((r, RET_QK_DIM), lambda b, i: (i, 0))
    return pl.pallas_call(
        functools.partial(_rope_kernel, scale=scale),
        grid=(bsz, t // r),
        in_specs=[pl.BlockSpec((1, r, width), lambda b, i: (b, i, col_blk)),
                  tab_spec, tab_spec, tab_spec],
        out_specs=pl.BlockSpec((1, r, width), lambda b, i: (b, i, 0)),
        out_shape=jax.ShapeDtypeStruct((bsz, t, width), BF16),
        name="rope",
        compiler_params=_cparams("parallel", "parallel"),
    )(pr, *tabs)


def _rope_tables(seq):
    rows = seq // GRID_W
    row = jnp.repeat(jnp.arange(rows), GRID_W).astype(F32)
    col = jnp.tile(jnp.arange(GRID_W), rows).astype(F32)
    q4 = RET_QK_DIM // 4
    freq = ROPE_BASE ** (-jnp.arange(q4, dtype=F32) / q4)
    ang = jnp.concatenate([row[:, None] * freq, row[:, None] * freq,
                           col[:, None] * freq, col[:, None] * freq], axis=1)
    first = (jnp.arange(RET_QK_DIM) % (2 * q4)) < q4
    cos, sin = jnp.cos(ang), jnp.sin(ang)
    sa = jnp.where(first, -sin, 0.0)
    sb = jnp.where(first, 0.0, sin)
    pad = lambda tab, v: jnp.concatenate([jnp.full((CTX_LEN, RET_QK_DIM), v, F32), tab], axis=0)
    return pad(cos, 1.0), pad(sa, 0.0), pad(sb, 0.0)


def _chunk_order(step, n_ctx, n_all, reverse):
    if not reverse:
        return step
    return jnp.where(step < n_ctx, n_ctx - 1 - step, n_all - 1 + n_ctx - step)


def _tri_mask(reverse):
    i = lax.broadcasted_iota(jnp.int32, (CHUNK, CHUNK), 0)
    j = lax.broadcasted_iota(jnp.int32, (CHUNK, CHUNK), 1)
    return (j >= i) if reverse else (j <= i)


def _ssd_kernel(xbc_ref, dt_ref, bias_ref, alog_ref, y_ref, h_ref, *, reverse, d_ssd, heads):
    hpg = heads // SSD_GROUPS
    gw = hpg * SSD_HEAD_DIM
    d = 1 if reverse else 0

    @pl.when(pl.program_id(1) == 0)
    def _():
        h_ref[...] = jnp.zeros_like(h_ref)

    mask = _tri_mask(reverse)
    dtv = _softplus(dt_ref[0] + bias_ref[...])
    la = dtv * (-jnp.exp(alog_ref[...]))
    acs = jnp.dot(mask.astype(F32), la, precision=lax.Precision.HIGHEST,
                  preferred_element_type=F32)
    acs_t = acs.T
    tot = acs[0:1, :] if reverse else acs[CHUNK - 1:CHUNK, :]
    to_end = jnp.exp(tot - acs)
    e_acs = jnp.exp(acs)
    e_tot = jnp.exp(tot)

    bm_off = d_ssd
    cm_off = d_ssd + SSD_GROUPS * SSD_STATE
    for g in range(SSD_GROUPS):
        cg = xbc_ref[0, :, cm_off + g * SSD_STATE:cm_off + (g + 1) * SSD_STATE]
        bg = xbc_ref[0, :, bm_off + g * SSD_STATE:bm_off + (g + 1) * SSD_STATE]
        scores = lax.dot_general(cg, bg, (((1,), (1,)), ((), ())), preferred_element_type=F32)
        hg = h_ref[g]
        inter = jnp.dot(cg, hg.astype(BF16), preferred_element_type=F32)
        xg = xbc_ref[0, :, g * gw:(g + 1) * gw].astype(F32)
        ys, vs, decs = [], [], []
        for r in range(hpg):
            hd = g * hpg + r
            c = d * heads + hd
            colf = acs[:, c:c + 1]
            dm = jnp.exp(jnp.where(mask, colf - acs_t[c:c + 1, :], NEG_BIG))
            p = (scores * dm).astype(BF16)
            v = xg[:, r * SSD_HEAD_DIM:(r + 1) * SSD_HEAD_DIM] * dtv[:, c:c + 1]
            y = jnp.dot(p, v.astype(BF16), preferred_element_type=F32)
            ys.append(y + e_acs[:, c:c + 1] * inter[:, r * SSD_HEAD_DIM:(r + 1) * SSD_HEAD_DIM])
            vs.append((v * to_end[:, c:c + 1]).astype(BF16))
            decs.append(jnp.broadcast_to(e_tot[:, c:c + 1], (1, SSD_HEAD_DIM)))
        y_ref[0, :, g * gw:(g + 1) * gw] = jnp.concatenate(ys, axis=1)
        upd = lax.dot_general(bg, jnp.concatenate(vs, axis=1), (((0,), (0,)), ((), ())),
                              preferred_element_type=F32)
        h_ref[g] = hg * jnp.concatenate(decs, axis=1) + upd


def _ssd_scan(xbc, dt, dt_bias, a_log, reverse):
    bsz, t, _ = xbc.shape
    heads = dt.shape[2] // 2
    d_ssd = heads * SSD_HEAD_DIM
    nc, n_ctx = t // CHUNK, CTX_LEN // CHUNK
    order = lambda b, s: (b, _chunk_order(s, n_ctx, nc, reverse), 0)
    return pl.pallas_call(
        functools.partial(_ssd_kernel, reverse=reverse, d_ssd=d_ssd, heads=heads),
        grid=(bsz, nc),
        in_specs=[pl.BlockSpec((1, CHUNK, xbc.shape[2]), order),
                  pl.BlockSpec((1, CHUNK, 2 * heads), order),
                  pl.BlockSpec((1, 2 * heads), lambda b, s: (0, 0)),
                  pl.BlockSpec((1, 2 * heads), lambda b, s: (0, 0))],
        out_specs=pl.BlockSpec((1, CHUNK, d_ssd), order),
        out_shape=jax.ShapeDtypeStruct((bsz, t, d_ssd), F32),
        scratch_shapes=[pltpu.VMEM((SSD_GROUPS, SSD_STATE, d_ssd // SSD_GROUPS), F32)],
        name="ssd_scan_bwd" if reverse else "ssd_scan_fwd",
        compiler_params=_cparams("parallel", "arbitrary"),
    )(xbc, dt, dt_bias.reshape(1, -1), a_log.reshape(1, -1))


def _ret_kernel(q_ref, k_ref, v_ref, y_ref, h_ref, *, reverse, v_dim):
    @pl.when(pl.program_id(1) == 0)
    def _():
        h_ref[...] = jnp.zeros_like(h_ref)

    mask = _tri_mask(reverse)
    i = lax.broadcasted_iota(jnp.int32, (CHUNK, CHUNK), 0)
    j = lax.broadcasted_iota(jnp.int32, (CHUNK, CHUNK), 1)
    dist = (jnp.abs(i - j)).astype(F32)
    n_in = ((CHUNK - i) if reverse else (i + 1)).astype(F32)
    n_out = (i if reverse else (CHUNK - 1 - i)).astype(F32)
    reps = v_dim // CHUNK
    for h in range(RET_HEADS):
        hh = RET_HEADS - 1 - h if reverse else h
        lg = math.log1p(-(2.0 ** (-5.0 - hh)))
        q = q_ref[0, :, h * RET_QK_DIM:(h + 1) * RET_QK_DIM]
        k = k_ref[0, :, h * RET_QK_DIM:(h + 1) * RET_QK_DIM]
        v = v_ref[0, :, h * v_dim:(h + 1) * v_dim].astype(BF16)
        scores = lax.dot_general(q, k, (((1,), (1,)), ((), ())), preferred_element_type=F32)
        p = (scores * jnp.where(mask, jnp.exp(lg * dist), 0.0)).astype(BF16)
        hs = h_ref[h]
        inter = jnp.dot(q, hs.astype(BF16), preferred_element_type=F32)
        e_in = jnp.concatenate([jnp.exp(lg * n_in)] * reps, axis=1)
        y_ref[0, :, h * v_dim:(h + 1) * v_dim] = (
            jnp.dot(p, v, preferred_element_type=F32) + e_in * inter)
        ks = (k.astype(F32) * jnp.exp(lg * n_out)).astype(BF16)
        upd = lax.dot_general(ks, v, (((0,), (0,)), ((), ())), preferred_element_type=F32)
        h_ref[h] = hs * math.exp(lg * CHUNK) + upd


def _ret_scan(q, k, pr, v_blk, v_width, reverse):
    bsz, t, qk_width = q.shape
    v_dim = v_width // RET_HEADS
    nc, n_ctx = t // CHUNK, CTX_LEN // CHUNK
    order = lambda b, s: (b, _chunk_order(s, n_ctx, nc, reverse), 0)
    return pl.pallas_call(
        functools.partial(_ret_kernel, reverse=reverse, v_dim=v_dim),
        grid=(bsz, nc),
        in_specs=[pl.BlockSpec((1, CHUNK, qk_width), order),
                  pl.BlockSpec((1, CHUNK, qk_width), order),
                  pl.BlockSpec((1, CHUNK, v_width),
                               lambda b, s: (b, _chunk_order(s, n_ctx, nc, reverse), v_blk))],
        out_specs=pl.BlockSpec((1, CHUNK, v_width), order),
        out_shape=jax.ShapeDtypeStruct((bsz, t, v_width), F32),
        scratch_shapes=[pltpu.VMEM((RET_HEADS, RET_QK_DIM, v_dim), F32)],
        name="ret_scan_bwd" if reverse else "ret_scan_fwd",
        compiler_params=_cparams("parallel", "arbitrary"),
    )(q, k, pr)


def _even_out_kernel(ysf_ref, ysb_ref, yrf_ref, yrb_ref, xs_ref, z_ref, g_ref,
                     dsk_ref, sn_ref, rn_ref, o_ref, *, d_ssd, v_dim):
    ys = ysf_ref[0] + ysb_ref[0] + dsk_ref[...] * xs_ref[0].astype(F32)
    ys = ys * _silu(z_ref[0])
    ys = ys * lax.rsqrt(jnp.mean(ys * ys, axis=-1, keepdims=True) + EPS) * sn_ref[...]
    o_ref[0, :, :d_ssd] = ys.astype(o_ref.dtype)
    for h in range(RET_HEADS):
        cols = slice(h * v_dim, (h + 1) * v_dim)
        yr = yrf_ref[0, :, cols] + yrb_ref[0, :, cols]
        mu = jnp.mean(yr, axis=-1, keepdims=True)
        dev = yr - mu
        var = jnp.mean(dev * dev, axis=-1, keepdims=True)
        yn = dev * lax.rsqrt(var + EPS) * rn_ref[:, cols]
        o_ref[0, :, d_ssd + h * v_dim:d_ssd + (h + 1) * v_dim] = (
            yn * _silu(g_ref[0, :, cols])).astype(o_ref.dtype)


def _even_out(ysf, ysb, yrf, yrb, xbc, pr, z_blk, g_blk, d_skip_row, ssd_norm, ret_norm):
    bsz, t, d_ssd = ysf.shape
    d_ret = yrf.shape[2]
    r = CHUNK
    blk = lambda w, c: pl.BlockSpec((1, r, w), lambda b, i: (b, i, c))
    row = lambda w: pl.BlockSpec((1, w), lambda b, i: (0, 0))
    return pl.pallas_call(
        functools.partial(_even_out_kernel, d_ssd=d_ssd, v_dim=d_ret // RET_HEADS),
        grid=(bsz, t // r),
        in_specs=[blk(d_ssd, 0), blk(d_ssd, 0), blk(d_ret, 0), blk(d_ret, 0),
                  blk(d_ssd, 0), blk(d_ssd, z_blk), blk(d_ret, g_blk),
                  row(d_ssd), row(d_ssd), row(d_ret)],
        out_specs=blk(d_ssd + d_ret, 0),
        out_shape=jax.ShapeDtypeStruct((bsz, t, d_ssd + d_ret), BF16),
        name="even_out",
        compiler_params=_cparams("parallel", "parallel"),
    )(ysf, ysb, yrf, yrb, xbc, pr, pr, d_skip_row, ssd_norm.reshape(1, -1),
      ret_norm.reshape(1, -1))


def _lru_kernel(*refs, reverse, n_tiles, final):
    if final:
        (x_ref, p_ref, n_ref, cw_ref, cb_ref, gw_ref, gb_ref, lam_ref, yf_ref, gate_ref,
         o_ref, carry_ref) = refs
    else:
        (x_ref, p_ref, n_ref, cw_ref, cb_ref, gw_ref, gb_ref, lam_ref, o_ref, carry_ref) = refs
    step = pl.program_id(2)
    tile = _chunk_order(step, CTX_LEN // ROW_TILE, n_tiles, reverse)

    @pl.when(step == 0)
    def _():
        carry_ref[...] = jnp.zeros_like(carry_ref)

    xf = _conv_taps(x_ref[0], p_ref[0], n_ref[0], cw_ref, cb_ref, tile, n_tiles)
    r_rows, cb = xf.shape
    xb = xf.astype(BF16)
    neg_c_sp = -LRU_C * _softplus(-lam_ref[0])
    a_parts, u_parts = [], []
    for kb in range(cb // RNN_BLOCK):
        cols = slice(kb * RNN_BLOCK, (kb + 1) * RNN_BLOCK)
        xk = xb[:, cols]
        rg = jax.nn.sigmoid(jnp.dot(xk, gw_ref[0, 0, kb], preferred_element_type=F32)
                            + gb_ref[0, 0:1, cols])
        ig = jax.nn.sigmoid(jnp.dot(xk, gw_ref[0, 1, kb], preferred_element_type=F32)
                            + gb_ref[0, 1:2, cols])
        a = jnp.exp(neg_c_sp[:, cols] * rg)
        a_parts.append(a)
        u_parts.append(jnp.sqrt(1.0 - a * a) * (ig * xf[:, cols]))
    a = jnp.concatenate(a_parts, axis=1)
    u = jnp.concatenate(u_parts, axis=1)

    row = lax.broadcasted_iota(jnp.int32, a.shape, 0)
    for s in (1, 2, 4):
        if reverse:
            inside = row < r_rows - s
            shift = r_rows - s
        else:
            inside = row >= s
            shift = s
        a_prev = jnp.where(inside, pltpu.roll(a, shift, 0), 1.0)
        u_prev = jnp.where(inside, pltpu.roll(u, shift, 0), 0.0)
        u = a * u_prev + u
        a = a * a_prev
    h = carry_ref[...]
    n_grp = r_rows // 8
    outs = [None] * n_grp
    for v in (range(n_grp - 1, -1, -1) if reverse else range(n_grp)):
        h = a[v * 8:(v + 1) * 8] * h + u[v * 8:(v + 1) * 8]
        outs[v] = h
    last = h[0:1] if reverse else h[7:8]
    carry_ref[...] = jnp.broadcast_to(last, carry_ref.shape)
    y = jnp.concatenate(outs, axis=0)
    if final:
        o_ref[0] = ((yf_ref[0] + y) * _silu(gate_ref[0])).astype(o_ref.dtype)
    else:
        o_ref[0] = y


def _lru_scan(pr, x_blk0, conv_w, conv_b, gate_w, gate_b, lam, reverse, yf=None, gate_blk0=0):
    bsz, t, _ = pr.shape
    d_rnn = conv_w.shape[1]
    cb = min(1024, d_rnn)
    r = ROW_TILE
    n_tiles = t // r
    d = 1 if reverse else 0
    final = yf is not None
    tile_of = lambda s: _chunk_order(s, CTX_LEN // r, n_tiles, reverse)
    rb = r // 8
    last = n_tiles * rb - 1
    in_specs = [
        pl.BlockSpec((1, r, cb), lambda b, j, s: (b, tile_of(s), x_blk0 + j)),
        pl.BlockSpec((1, 8, cb), lambda b, j, s: (b, jnp.maximum(tile_of(s) * rb - 1, 0), x_blk0 + j)),
        pl.BlockSpec((1, 8, cb), lambda b, j, s: (b, jnp.minimum((tile_of(s) + 1) * rb, last), x_blk0 + j)),
        pl.BlockSpec((CONV_W, cb), lambda b, j, s: (0, j)),
        pl.BlockSpec((1, cb), lambda b, j, s: (0, j)),
        pl.BlockSpec((1, 2, cb // RNN_BLOCK, RNN_BLOCK, RNN_BLOCK), lambda b, j, s: (d, 0, j, 0, 0)),
        pl.BlockSpec((1, 2, cb), lambda b, j, s: (d, 0, j)),
        pl.BlockSpec((1, 1, cb), lambda b, j, s: (d, 0, j)),
    ]
    args = [pr, pr, pr, conv_w, conv_b.reshape(1, d_rnn), gate_w, gate_b, lam.reshape(2, 1, d_rnn)]
    if final:
        in_specs += [pl.BlockSpec((1, r, cb), lambda b, j, s: (b, tile_of(s), j)),
                     pl.BlockSpec((1, r, cb), lambda b, j, s: (b, tile_of(s), gate_blk0 + j))]
        args += [yf, pr]
    return pl.pallas_call(
        functools.partial(_lru_kernel, reverse=reverse, n_tiles=n_tiles, final=final),
        grid=(bsz, d_rnn // cb, n_tiles),
        in_specs=in_specs,
        out_specs=pl.BlockSpec((1, r, cb), lambda b, j, s: (b, tile_of(s), j)),
        out_shape=jax.ShapeDtypeStruct((bsz, t, d_rnn), BF16 if final else F32),
        scratch_shapes=[pltpu.VMEM((8, cb), F32)],
        name="lru_scan_bwd" if reverse else "lru_scan_fwd",
        compiler_params=_cparams("parallel", "parallel", "arbitrary"),
    )(*args)


def _row_tile_for(m):
    for tm in (512, 256):
        if m % tm == 0:
            return tm
    raise ValueError(m)


def _even_layer(h2, bsz, t, w_in, conv_w, conv_b, dt_bias, a_log, d_skip, ssd_norm, ret_norm,
                rope_tabs):
    d_model = h2.shape[1]
    d_ssd = d_model
    d_ret = d_model
    gn = SSD_GROUPS * SSD_STATE
    heads = d_ssd // SSD_HEAD_DIM
    qk = RET_HEADS * RET_QK_DIM
    z_w, xs_w, bm_w, cm_w, dt_w, q_w, k_w, v_w, g_w = jnp.split(
        w_in, np.cumsum([d_ssd, d_ssd, gn, gn, 2 * heads, qk, qk, d_ret])[:].tolist(), axis=1)
    w_main = jnp.concatenate([z_w, xs_w, v_w, g_w, q_w, k_w, bm_w, cm_w], axis=1).astype(BF16)
    tm = _row_tile_for(h2.shape[0])
    pr = _matmul(h2, w_main, tm, 512).reshape(bsz, t, -1)
    dt = _matmul(h2, dt_w.astype(BF16), tm, 2 * heads).reshape(bsz, t, 2 * heads)
    z_blk, xs_blk, v_blk, g_blk = 0, 1, 2, 3
    q_off, k_off, bc_off = 4 * d_model, 4 * d_model + qk, 4 * d_model + 2 * qk

    cw = min(512, gn)
    n_xs = d_ssd // cw
    col_map = lambda j: jnp.where(j < n_xs, xs_blk * n_xs + j, bc_off // cw + j - n_xs)
    xbc = _conv_silu(pr, conv_w, conv_b, col_map, d_ssd + 2 * gn, cw)
    q = _rope(pr, q_off // qk, qk, rope_tabs, 1.0)
    k = _rope(pr, k_off // qk, qk, rope_tabs, RET_QK_DIM ** -0.5)

    ysf = _ssd_scan(xbc, dt, dt_bias, a_log, False)
    ysb = _ssd_scan(xbc, dt, dt_bias, a_log, True)
    yrf = _ret_scan(q, k, pr, v_blk, d_ret, False)
    yrb = _ret_scan(q, k, pr, v_blk, d_ret, True)
    d_skip_row = jnp.repeat(d_skip, SSD_HEAD_DIM).reshape(1, d_ssd)
    return _even_out(ysf, ysb, yrf, yrb, xbc, pr, z_blk, g_blk, d_skip_row, ssd_norm, ret_norm)


def _odd_layer(h2, bsz, t, w_in, conv_w, conv_b, gate_w, gate_b, lam):
    d_rnn = conv_w.shape[1]
    tm = _row_tile_for(h2.shape[0])
    pr = _matmul(h2, w_in.astype(BF16), tm, 512).reshape(bsz, t, 2 * d_rnn)
    cb = min(1024, d_rnn)
    gw = gate_w.astype(BF16)
    yf = _lru_scan(pr, d_rnn // cb, conv_w, conv_b, gw, gate_b, lam, False)
    return _lru_scan(pr, d_rnn // cb, conv_w, conv_b, gw, gate_b, lam, True, yf=yf, gate_blk0=0)


def kernel(x, c, ctx, c_ctx, ada_w, ada_b, norm_pre, norm_post, e_w_in, e_conv_w, e_conv_b,
           e_dt_bias, e_a_log, e_d_skip, e_ssd_norm, e_ret_norm, e_w_out, o_w_in, o_conv_w,
           o_conv_b, o_gate_w, o_gate_b, o_lambda, o_w_out):
    bsz, seq, d = x.shape
    depth = ada_w.shape[0]
    t = CTX_LEN + seq
    m = bsz * t
    xs = jnp.concatenate([ctx, x], axis=1).reshape(m, d)

    mod = _modulation(jnp.concatenate([c, c_ctx[None]], axis=0), ada_w, ada_b)
    n_ctx_blk, n_lat_blk = CTX_LEN // ROW_TILE, seq // ROW_TILE
    mod_lat = jnp.broadcast_to(mod[:, :bsz].reshape(depth, bsz, 1, 3, d),
                               (depth, bsz, n_lat_blk, 3, d))
    mod_ctx = jnp.broadcast_to(mod[:, bsz].reshape(depth, 1, 1, 3, d),
                               (depth, bsz, n_ctx_blk, 3, d))
    modblk = jnp.concatenate([mod_ctx, mod_lat], axis=2).reshape(depth, m // ROW_TILE, 3, d)

    rope_tabs = _rope_tables(seq)
    tm = _row_tile_for(m)
    for i in range(depth):
        j = i // 2
        h2 = _prenorm(xs, norm_pre[i], modblk[i])
        if i % 2 == 0:
            y = _even_layer(h2, bsz, t, e_w_in[j], e_conv_w[j], e_conv_b[j], e_dt_bias[j],
                            e_a_log[j], e_d_skip[j], e_ssd_norm[j], e_ret_norm[j], rope_tabs)
            w_out = e_w_out[j]
        else:
            y = _odd_layer(h2, bsz, t, o_w_in[j], o_conv_w[j], o_conv_b[j], o_gate_w[j],
                           o_gate_b[j], o_lambda[j])
            w_out = o_w_out[j]
        xs = _outproj(y.reshape(m, -1), w_out.astype(BF16), xs, modblk[i], norm_post[i], tm, 512)
    return xs.reshape(bsz, t, d)[:, CTX_LEN:]
```

```python
import functools
import math

import numpy as np
import jax
import jax.numpy as jnp
from jax import lax
from jax.experimental import pallas as pl
from jax.experimental.pallas import tpu as pltpu

F32 = jnp.float32
BF16 = jnp.bfloat16

GRID_W = 64
CTX_LEN = 256
SSD_HEAD_DIM = 64
SSD_GROUPS = 8
SSD_STATE = 128
RET_HEADS = 16
RET_QK_DIM = 128
RNN_BLOCK = 256
LRU_C = 8.0
CONV_W = 4
CHUNK = 128
ROPE_BASE = 10000.0
EPS = 1e-6

LANES = 128
ROW_TILE = 256
VMEM_LIMIT = 56 * 1024 * 1024
NEG_BIG = -1e30


def _cparams(*sem):
    return pltpu.CompilerParams(dimension_semantics=sem, vmem_limit_bytes=VMEM_LIMIT)


def _silu(x):
    return x * jax.nn.sigmoid(x)


def _softplus(x):
    return jnp.maximum(x, 0.0) + jnp.log1p(jnp.exp(-jnp.abs(x)))


def _mod_kernel(ct_ref, w_ref, b_ref, o_ref, *, n_vec):
    w = w_ref[0]
    o_ref[0] = jnp.zeros(o_ref.shape[1:], F32)
    for m in range(n_vec):
        col = _silu(ct_ref[:, m:m + 1])
        o_ref[0, m:m + 1, :] = jnp.sum(w * col, axis=0, keepdims=True) + b_ref[0]


def _modulation(cvecs, ada_w, ada_b):
    depth, d, n3 = ada_w.shape
    n_vec = cvecs.shape[0]
    ct = jnp.zeros((d, 8), F32).at[:, :n_vec].set(cvecs.T)
    tn = 512
    return pl.pallas_call(
        functools.partial(_mod_kernel, n_vec=n_vec),
        grid=(depth, n3 // tn),
        in_specs=[pl.BlockSpec((d, 8), lambda i, j: (0, 0)),
                  pl.BlockSpec((1, d, tn), lambda i, j: (i, 0, j)),
                  pl.BlockSpec((1, 1, tn), lambda i, j: (i, 0, j))],
        out_specs=pl.BlockSpec((1, 8, tn), lambda i, j: (i, 0, j)),
        out_shape=jax.ShapeDtypeStruct((depth, 8, n3), F32),
        name="modulation",
        compiler_params=_cparams("parallel", "parallel"),
    )(ct, ada_w, ada_b.reshape(depth, 1, n3))


def _prenorm_kernel(x_ref, g_ref, mod_ref, o_ref):
    x = x_ref[...]
    y = x * lax.rsqrt(jnp.mean(x * x, axis=-1, keepdims=True) + EPS) * g_ref[...]
    o_ref[...] = (y * (1.0 + mod_ref[0, 1:2, :]) + mod_ref[0, 0:1, :]).astype(o_ref.dtype)


def _prenorm(x2, g, modblk):
    m, d = x2.shape
    r = ROW_TILE
    return pl.pallas_call(
        _prenorm_kernel,
        grid=(m // r,),
        in_specs=[pl.BlockSpec((r, d), lambda i: (i, 0)),
                  pl.BlockSpec((1, d), lambda i: (0, 0)),
                  pl.BlockSpec((1, 3, d), lambda i: (i, 0, 0))],
        out_specs=pl.BlockSpec((r, d), lambda i: (i, 0)),
        out_shape=jax.ShapeDtypeStruct((m, d), BF16),
        name="prenorm",
        compiler_params=_cparams("parallel"),
    )(x2, g.reshape(1, d), modblk)


def _residual(x, y, gate, g_post):
    return x + gate * (y * lax.rsqrt(jnp.mean(y * y, axis=-1, keepdims=True) + EPS) * g_post)


def _resnorm_kernel(x_ref, y_ref, modp_ref, gpost_ref, gpre_ref, modn_ref, xn_ref, h_ref):
    xn = _residual(x_ref[...], y_ref[...], modp_ref[0, 2:3, :], gpost_ref[...])
    xn_ref[...] = xn
    hn = xn * lax.rsqrt(jnp.mean(xn * xn, axis=-1, keepdims=True) + EPS) * gpre_ref[...]
    h_ref[...] = (hn * (1.0 + modn_ref[0, 1:2, :]) + modn_ref[0, 0:1, :]).astype(h_ref.dtype)


def _resnorm(x2, y2, modp, g_post, g_pre, modn):
    m, d = x2.shape
    r = ROW_TILE
    tile = pl.BlockSpec((r, d), lambda i: (i, 0))
    vec = pl.BlockSpec((1, d), lambda i: (0, 0))
    mod = pl.BlockSpec((1, 3, d), lambda i: (i, 0, 0))
    return pl.pallas_call(
        _resnorm_kernel,
        grid=(m // r,),
        in_specs=[tile, tile, mod, vec, vec, mod],
        out_specs=[tile, tile],
        out_shape=[jax.ShapeDtypeStruct((m, d), F32), jax.ShapeDtypeStruct((m, d), BF16)],
        name="resnorm",
        compiler_params=_cparams("parallel"),
    )(x2, y2, modp, g_post.reshape(1, d), g_pre.reshape(1, d), modn)


def _final_kernel(x_ref, y_ref, modp_ref, gpost_ref, o_ref):
    o_ref[0] = _residual(x_ref[0], y_ref[0], modp_ref[0, 2:3, :], gpost_ref[...])


def _final_residual(x3, y3, modp, g_post, seq):
    bsz, t, d = x3.shape
    r = ROW_TILE
    skip, per = (t - seq) // r, t // r
    tile = pl.BlockSpec((1, r, d), lambda b, i: (b, i + skip, 0))
    return pl.pallas_call(
        _final_kernel,
        grid=(bsz, seq // r),
        in_specs=[tile, tile,
                  pl.BlockSpec((1, 3, d), lambda b, i: (b * per + i + skip, 0, 0)),
                  pl.BlockSpec((1, d), lambda b, i: (0, 0))],
        out_specs=pl.BlockSpec((1, r, d), lambda b, i: (b, i, 0)),
        out_shape=jax.ShapeDtypeStruct((bsz, seq, d), F32),
        name="final_residual",
        compiler_params=_cparams("parallel", "parallel"),
    )(x3, y3, modp, g_post.reshape(1, d))


def _mm_kernel(a_ref, b_ref, o_ref):
    o_ref[...] = jnp.dot(a_ref[...], b_ref[...], preferred_element_type=F32).astype(o_ref.dtype)


A_TILE_BYTES = 12 * 1024 * 1024


def _matmul(a, b, tn, out_dtype=F32):
    m, k = a.shape
    n = b.shape[1]
    tm = next(c for c in (1536, 1024, 768, 512, 256)
              if m % c == 0 and c * k * a.dtype.itemsize <= A_TILE_BYTES)
    return pl.pallas_call(
        _mm_kernel,
        grid=(m // tm, n // tn),
        in_specs=[pl.BlockSpec((tm, k), lambda i, j: (i, 0)),
                  pl.BlockSpec((k, tn), lambda i, j: (0, j))],
        out_specs=pl.BlockSpec((tm, tn), lambda i, j: (i, j)),
        out_shape=jax.ShapeDtypeStruct((m, n), out_dtype),
        name="proj_k%d" % k,
        compiler_params=_cparams("parallel", "arbitrary"),
    )(a, b)


def _conv_taps(x, prev8, next8, w_ref, b_ref, tile, n_tiles, pad_ref):
    r = x.shape[0]
    ctx_tiles = CTX_LEN // r
    prev_ok = jnp.logical_and(tile != 0, tile != ctx_tiles)
    next_ok = jnp.logical_and(tile != ctx_tiles - 1, tile != n_tiles - 1)
    pad_ref[0:8] = jnp.where(prev_ok, prev8, 0.0)
    pad_ref[8:8 + r] = x
    pad_ref[8 + r:16 + r] = jnp.where(next_ok, next8, 0.0)
    return (b_ref[...] + w_ref[0:1, :] * pad_ref[6:6 + r] + w_ref[1:2, :] * pad_ref[7:7 + r]
            + w_ref[2:3, :] * x + w_ref[3:4, :] * pad_ref[9:9 + r])


def _halo_specs(r, cw, col_map, n_tiles):
    rb = r // 8
    last = n_tiles * rb - 1
    return [
        pl.BlockSpec((1, r, cw), lambda b, t, j: (b, t, col_map(j))),
        pl.BlockSpec((1, 8, cw), lambda b, t, j: (b, jnp.maximum(t * rb - 1, 0), col_map(j))),
        pl.BlockSpec((1, 8, cw), lambda b, t, j: (b, jnp.minimum((t + 1) * rb, last), col_map(j))),
    ]


def _conv_silu_kernel(x_ref, p_ref, n_ref, w_ref, b_ref, o_ref, pad_ref, *, n_tiles):
    y = _conv_taps(x_ref[0], p_ref[0], n_ref[0], w_ref, b_ref, pl.program_id(1), n_tiles, pad_ref)
    o_ref[0] = _silu(y).astype(o_ref.dtype)


def _conv_silu(pr, conv_w, conv_b, col_map, n_cols, cw):
    bsz, t, _ = pr.shape
    r = ROW_TILE
    n_tiles = t // r
    return pl.pallas_call(
        functools.partial(_conv_silu_kernel, n_tiles=n_tiles),
        grid=(bsz, n_tiles, n_cols // cw),
        in_specs=_halo_specs(r, cw, col_map, n_tiles) + [
            pl.BlockSpec((CONV_W, cw), lambda b, t, j: (0, j)),
            pl.BlockSpec((1, cw), lambda b, t, j: (0, j))],
        out_specs=pl.BlockSpec((1, r, cw), lambda b, t, j: (b, t, j)),
        out_shape=jax.ShapeDtypeStruct((bsz, t, n_cols), BF16),
        scratch_shapes=[pltpu.VMEM((r + 16, cw), F32)],
        name="conv_silu",
        compiler_params=_cparams("parallel", "parallel", "parallel"),
    )(pr, pr, pr, conv_w, conv_b.reshape(1, n_cols))


def _rope_kernel(x_ref, cos_ref, sa_ref, sb_ref, o_ref, *, scale):
    cos, sa, sb = cos_ref[...], sa_ref[...], sb_ref[...]
    q4 = RET_QK_DIM // 4
    for h in range(x_ref.shape[2] // RET_QK_DIM):
        cols = slice(h * RET_QK_DIM, (h + 1) * RET_QK_DIM)
        x = x_ref[0, :, cols]
        y = x * cos + pltpu.roll(x, RET_QK_DIM - q4, 1) * sa + pltpu.roll(x, q4, 1) * sb
        o_ref[0, :, cols] = (y * scale).astype(o_ref.dtype)


def _rope(pr, col_blk, width, tabs, scale):
    bsz, t, _ = pr.shape
    r = ROW_TILE
    tab_spec = pl.BlockSpec((r, RET_QK_DIM), lambda b, i: (i, 0))
    return pl.pallas_call(
        functools.partial(_rope_kernel, scale=scale),
        grid=(bsz, t // r),
        in_specs=[pl.BlockSpec((1, r, width), lambda b, i: (b, i, col_blk)),
                  tab_spec, tab_spec, tab_spec],
        out_specs=pl.BlockSpec((1, r, width), lambda b, i: (b, i, 0)),
        out_shape=jax.ShapeDtypeStruct((bsz, t, width), BF16),
        name="rope",
        compiler_params=_cparams("parallel", "parallel"),
    )(pr, *tabs)


def _rope_tables(seq):
    rows = seq // GRID_W
    row = jnp.repeat(jnp.arange(rows), GRID_W).astype(F32)
    col = jnp.tile(jnp.arange(GRID_W), rows).astype(F32)
    q4 = RET_QK_DIM // 4
    freq = ROPE_BASE ** (-jnp.arange(q4, dtype=F32) / q4)
    ang = jnp.concatenate([row[:, None] * freq, row[:, None] * freq,
                           col[:, None] * freq, col[:, None] * freq], axis=1)
    first = (jnp.arange(RET_QK_DIM) % (2 * q4)) < q4
    cos, sin = jnp.cos(ang), jnp.sin(ang)
    sa = jnp.where(first, -sin, 0.0)
    sb = jnp.where(first, 0.0, sin)
    pad = lambda tab, v: jnp.concatenate([jnp.full((CTX_LEN, RET_QK_DIM), v, F32), tab], axis=0)
    return pad(cos, 1.0), pad(sa, 0.0), pad(sb, 0.0)


def _chunk_order(step, n_ctx, n_all, reverse):
    if not reverse:
        return step
    return jnp.where(step < n_ctx, n_ctx - 1 - step, n_all - 1 + n_ctx - step)


def _tri_mask(reverse):
    i = lax.broadcasted_iota(jnp.int32, (CHUNK, CHUNK), 0)
    j = lax.broadcasted_iota(jnp.int32, (CHUNK, CHUNK), 1)
    return (j >= i) if reverse else (j <= i)


def _ssd_kernel(xbc_ref, dt_ref, bias_ref, alog_ref, y_ref, h_ref, *, reverse, d_ssd, heads):
    hpg = heads // SSD_GROUPS
    gw = hpg * SSD_HEAD_DIM
    d = 1 if reverse else 0

    @pl.when(pl.program_id(1) == 0)
    def _():
        h_ref[...] = jnp.zeros_like(h_ref)

    mask = _tri_mask(reverse)
    hpt = LANES // SSD_HEAD_DIM
    lane_head = lax.broadcasted_iota(jnp.int32, (CHUNK, LANES), 1) // SSD_HEAD_DIM
    lane_head_row = lax.broadcasted_iota(jnp.int32, (1, LANES), 1) // SSD_HEAD_DIM
    lo = d * heads
    dtv = _softplus(dt_ref[0] + bias_ref[...])[:, lo:lo + heads]
    la = dtv * (-jnp.exp(alog_ref[...]))[:, lo:lo + heads]
    acs = jnp.dot(mask.astype(F32), la, precision=lax.Precision.HIGHEST,
                  preferred_element_type=F32)
    tot = acs[0:1, :] if reverse else acs[CHUNK - 1:CHUNK, :]
    e_tot = jnp.exp(tot)
    w = jnp.exp(tot - acs) * dtv
    acs_w_t = jnp.concatenate([acs, w], axis=1).T
    acs_t, w_t = acs_w_t[:heads], acs_w_t[heads:]
    dt_t = jnp.concatenate([dtv, dtv], axis=1).T[:heads]

    bm_off = d_ssd
    cm_off = d_ssd + SSD_GROUPS * SSD_STATE
    for g in range(SSD_GROUPS):
        cg = xbc_ref[0, :, cm_off + g * SSD_STATE:cm_off + (g + 1) * SSD_STATE]
        bg = xbc_ref[0, :, bm_off + g * SSD_STATE:bm_off + (g + 1) * SSD_STATE]
        scores = lax.dot_general(cg, bg, (((1,), (1,)), ((), ())), preferred_element_type=F32)
        cg32 = cg.astype(F32)
        bg_t = bg.astype(F32).T
        hg = h_ref[g]
        hg16 = hg.astype(BF16)
        xg = xbc_ref[0, :, g * gw:(g + 1) * gw]
        for lt in range(gw // LANES):
            cols = slice(lt * LANES, (lt + 1) * LANES)
            xl, hl = xg[:, cols], hg16[:, cols]
            y = upd = dec = None
            for sub in range(hpt):
                hd = g * hpg + lt * hpt + sub
                col_b = jnp.broadcast_to(acs[:, hd:hd + 1], (CHUNK, CHUNK))
                dm = jnp.exp(jnp.where(mask, col_b - acs_t[hd:hd + 1, :], NEG_BIG))
                p = (scores * dm * dt_t[hd:hd + 1, :]).astype(BF16)
                c_in = (cg32 * jnp.exp(col_b)).astype(BF16)
                y_h = (jnp.dot(p, xl, preferred_element_type=F32)
                       + jnp.dot(c_in, hl, preferred_element_type=F32))
                b_out = (bg_t * w_t[hd:hd + 1, :]).astype(BF16)
                upd_h = jnp.dot(b_out, xl, preferred_element_type=F32)
                dec_h = jnp.broadcast_to(e_tot[:, hd:hd + 1], (1, LANES))
                mine = lane_head == sub
                y = y_h if y is None else jnp.where(mine, y_h, y)
                upd = upd_h if upd is None else jnp.where(mine, upd_h, upd)
                dec = dec_h if dec is None else jnp.where(lane_head_row == sub, dec_h, dec)
            y_ref[0, :, g * gw + lt * LANES:g * gw + (lt + 1) * LANES] = y
            h_ref[g, :, cols] = hg[:, cols] * dec + upd


def _ssd_scan(xbc, dt, dt_bias, a_log, reverse):
    bsz, t, _ = xbc.shape
    heads = dt.shape[2] // 2
    d_ssd = heads * SSD_HEAD_DIM
    nc, n_ctx = t // CHUNK, CTX_LEN // CHUNK
    order = lambda b, s: (b, _chunk_order(s, n_ctx, nc, reverse), 0)
    return pl.pallas_call(
        functools.partial(_ssd_kernel, reverse=reverse, d_ssd=d_ssd, heads=heads),
        grid=(bsz, nc),
        in_specs=[pl.BlockSpec((1, CHUNK, xbc.shape[2]), order),
                  pl.BlockSpec((1, CHUNK, 2 * heads), order),
                  pl.BlockSpec((1, 2 * heads), lambda b, s: (0, 0)),
                  pl.BlockSpec((1, 2 * heads), lambda b, s: (0, 0))],
        out_specs=pl.BlockSpec((1, CHUNK, d_ssd), order),
        out_shape=jax.ShapeDtypeStruct((bsz, t, d_ssd), F32),
        scratch_shapes=[pltpu.VMEM((SSD_GROUPS, SSD_STATE, d_ssd // SSD_GROUPS), F32)],
        name="ssd_scan_bwd" if reverse else "ssd_scan_fwd",
        compiler_params=_cparams("parallel", "arbitrary"),
    )(xbc, dt, dt_bias.reshape(1, -1), a_log.reshape(1, -1))


def _ret_kernel(q_ref, k_ref, v_ref, y_ref, h_ref, *, reverse, v_dim):
    @pl.when(pl.program_id(1) == 0)
    def _():
        h_ref[...] = jnp.zeros_like(h_ref)

    mask = _tri_mask(reverse)
    i = lax.broadcasted_iota(jnp.int32, (CHUNK, CHUNK), 0)
    j = lax.broadcasted_iota(jnp.int32, (CHUNK, CHUNK), 1)
    dist = (jnp.abs(i - j)).astype(F32)
    n_in = ((CHUNK - i) if reverse else (i + 1)).astype(F32)
    n_out = (i if reverse else (CHUNK - 1 - i)).astype(F32)
    reps = v_dim // CHUNK
    for h in range(RET_HEADS):
        hh = RET_HEADS - 1 - h if reverse else h
        lg = math.log1p(-(2.0 ** (-5.0 - hh)))
        q = q_ref[0, :, h * RET_QK_DIM:(h + 1) * RET_QK_DIM]
        k = k_ref[0, :, h * RET_QK_DIM:(h + 1) * RET_QK_DIM]
        v = v_ref[0, :, h * v_dim:(h + 1) * v_dim].astype(BF16)
        scores = lax.dot_general(q, k, (((1,), (1,)), ((), ())), preferred_element_type=F32)
        p = (scores * jnp.where(mask, jnp.exp(lg * dist), 0.0)).astype(BF16)
        hs = h_ref[h]
        inter = jnp.dot(q, hs.astype(BF16), preferred_element_type=F32)
        e_in = jnp.concatenate([jnp.exp(lg * n_in)] * reps, axis=1)
        y_ref[0, :, h * v_dim:(h + 1) * v_dim] = (
            jnp.dot(p, v, preferred_element_type=F32) + e_in * inter)
        ks = (k.astype(F32) * jnp.exp(lg * n_out)).astype(BF16)
        upd = lax.dot_general(ks, v, (((0,), (0,)), ((), ())), preferred_element_type=F32)
        h_ref[h] = hs * math.exp(lg * CHUNK) + upd


def _ret_scan(q, k, pr, v_blk, v_width, reverse):
    bsz, t, qk_width = q.shape
    v_dim = v_width // RET_HEADS
    nc, n_ctx = t // CHUNK, CTX_LEN // CHUNK
    order = lambda b, s: (b, _chunk_order(s, n_ctx, nc, reverse), 0)
    return pl.pallas_call(
        functools.partial(_ret_kernel, reverse=reverse, v_dim=v_dim),
        grid=(bsz, nc),
        in_specs=[pl.BlockSpec((1, CHUNK, qk_width), order),
                  pl.BlockSpec((1, CHUNK, qk_width), order),
                  pl.BlockSpec((1, CHUNK, v_width),
                               lambda b, s: (b, _chunk_order(s, n_ctx, nc, reverse), v_blk))],
        out_specs=pl.BlockSpec((1, CHUNK, v_width), order),
        out_shape=jax.ShapeDtypeStruct((bsz, t, v_width), F32),
        scratch_shapes=[pltpu.VMEM((RET_HEADS, RET_QK_DIM, v_dim), F32)],
        name="ret_scan_bwd" if reverse else "ret_scan_fwd",
        compiler_params=_cparams("parallel", "arbitrary"),
    )(q, k, pr)


def _even_out_kernel(ysf_ref, ysb_ref, yrf_ref, yrb_ref, xs_ref, z_ref, g_ref,
                     dsk_ref, sn_ref, rn_ref, o_ref, *, d_ssd, v_dim):
    ys = ysf_ref[0] + ysb_ref[0] + dsk_ref[...] * xs_ref[0].astype(F32)
    ys = ys * _silu(z_ref[0])
    ys = ys * lax.rsqrt(jnp.mean(ys * ys, axis=-1, keepdims=True) + EPS) * sn_ref[...]
    o_ref[0, :, :d_ssd] = ys.astype(o_ref.dtype)
    for h in range(RET_HEADS):
        cols = slice(h * v_dim, (h + 1) * v_dim)
        yr = yrf_ref[0, :, cols] + yrb_ref[0, :, cols]
        mu = jnp.mean(yr, axis=-1, keepdims=True)
        dev = yr - mu
        var = jnp.mean(dev * dev, axis=-1, keepdims=True)
        yn = dev * lax.rsqrt(var + EPS) * rn_ref[:, cols]
        o_ref[0, :, d_ssd + h * v_dim:d_ssd + (h + 1) * v_dim] = (
            yn * _silu(g_ref[0, :, cols])).astype(o_ref.dtype)


def _even_out(ysf, ysb, yrf, yrb, xbc, pr, z_blk, g_blk, d_skip_row, ssd_norm, ret_norm):
    bsz, t, d_ssd = ysf.shape
    d_ret = yrf.shape[2]
    r = CHUNK
    blk = lambda w, c: pl.BlockSpec((1, r, w), lambda b, i: (b, i, c))
    row = lambda w: pl.BlockSpec((1, w), lambda b, i: (0, 0))
    return pl.pallas_call(
        functools.partial(_even_out_kernel, d_ssd=d_ssd, v_dim=d_ret // RET_HEADS),
        grid=(bsz, t // r),
        in_specs=[blk(d_ssd, 0), blk(d_ssd, 0), blk(d_ret, 0), blk(d_ret, 0),
                  blk(d_ssd, 0), blk(d_ssd, z_blk), blk(d_ret, g_blk),
                  row(d_ssd), row(d_ssd), row(d_ret)],
        out_specs=blk(d_ssd + d_ret, 0),
        out_shape=jax.ShapeDtypeStruct((bsz, t, d_ssd + d_ret), BF16),
        name="even_out",
        compiler_params=_cparams("parallel", "parallel"),
    )(ysf, ysb, yrf, yrb, xbc, pr, pr, d_skip_row, ssd_norm.reshape(1, -1),
      ret_norm.reshape(1, -1))


def _lru_kernel(*refs, reverse, n_tiles, final):
    if final:
        (x_ref, p_ref, n_ref, cw_ref, cb_ref, gw_ref, gb_ref, lam_ref, yf_ref, gate_ref,
         o_ref, carry_ref, pad_ref, as_ref, us_ref) = refs
    else:
        (x_ref, p_ref, n_ref, cw_ref, cb_ref, gw_ref, gb_ref, lam_ref,
         o_ref, carry_ref, pad_ref, as_ref, us_ref) = refs
    step = pl.program_id(2)
    tile = _chunk_order(step, CTX_LEN // ROW_TILE, n_tiles, reverse)

    @pl.when(step == 0)
    def _():
        carry_ref[...] = jnp.zeros_like(carry_ref)

    xf = _conv_taps(x_ref[0], p_ref[0], n_ref[0], cw_ref, cb_ref, tile, n_tiles, pad_ref)
    r_rows, cb = xf.shape
    xb = xf.astype(BF16)
    neg_c_sp = -LRU_C * _softplus(-lam_ref[0])
    a_parts, u_parts = [], []
    for kb in range(cb // RNN_BLOCK):
        cols = slice(kb * RNN_BLOCK, (kb + 1) * RNN_BLOCK)
        xk = xb[:, cols]
        rg = jax.nn.sigmoid(jnp.dot(xk, gw_ref[0, 0, kb], preferred_element_type=F32)
                            + gb_ref[0, 0:1, cols])
        ig = jax.nn.sigmoid(jnp.dot(xk, gw_ref[0, 1, kb], preferred_element_type=F32)
                            + gb_ref[0, 1:2, cols])
        a = jnp.exp(neg_c_sp[:, cols] * rg)
        a_parts.append(a)
        u_parts.append(jnp.sqrt(1.0 - a * a) * (ig * xf[:, cols]))
    a = jnp.concatenate(a_parts, axis=1)
    u = jnp.concatenate(u_parts, axis=1)

    body = 0 if reverse else 8
    pad = r_rows if reverse else 0
    as_ref[pad:pad + 8] = jnp.ones((8, cb), F32)
    us_ref[pad:pad + 8] = jnp.zeros((8, cb), F32)
    for s in (1, 2, 4):
        as_ref[body:body + r_rows] = a
        us_ref[body:body + r_rows] = u
        src = body + s if reverse else body - s
        u = a * us_ref[src:src + r_rows] + u
        a = a * as_ref[src:src + r_rows]
    h = carry_ref[...]
    n_grp = r_rows // 8
    outs = [None] * n_grp
    for v in (range(n_grp - 1, -1, -1) if reverse else range(n_grp)):
        h = a[v * 8:(v + 1) * 8] * h + u[v * 8:(v + 1) * 8]
        outs[v] = h
    last = h[0:1] if reverse else h[7:8]
    carry_ref[...] = jnp.broadcast_to(last, carry_ref.shape)
    y = jnp.concatenate(outs, axis=0)
    if final:
        o_ref[0] = ((yf_ref[0] + y) * _silu(gate_ref[0])).astype(o_ref.dtype)
    else:
        o_ref[0] = y


def _lru_scan(pr, x_blk0, conv_w, conv_b, gate_w, gate_b, lam, reverse, yf=None, gate_blk0=0):
    bsz, t, _ = pr.shape
    d_rnn = conv_w.shape[1]
    cb = min(1024, d_rnn)
    r = ROW_TILE
    n_tiles = t // r
    d = 1 if reverse else 0
    final = yf is not None
    tile_of = lambda s: _chunk_order(s, CTX_LEN // r, n_tiles, reverse)
    rb = r // 8
    last = n_tiles * rb - 1
    in_specs = [
        pl.BlockSpec((1, r, cb), lambda b, j, s: (b, tile_of(s), x_blk0 + j)),
        pl.BlockSpec((1, 8, cb), lambda b, j, s: (b, jnp.maximum(tile_of(s) * rb - 1, 0), x_blk0 + j)),
        pl.BlockSpec((1, 8, cb), lambda b, j, s: (b, jnp.minimum((tile_of(s) + 1) * rb, last), x_blk0 + j)),
        pl.BlockSpec((CONV_W, cb), lambda b, j, s: (0, j)),
        pl.BlockSpec((1, cb), lambda b, j, s: (0, j)),
        pl.BlockSpec((1, 2, cb // RNN_BLOCK, RNN_BLOCK, RNN_BLOCK), lambda b, j, s: (d, 0, j, 0, 0)),
        pl.BlockSpec((1, 2, cb), lambda b, j, s: (d, 0, j)),
        pl.BlockSpec((1, 1, cb), lambda b, j, s: (d, 0, j)),
    ]
    args = [pr, pr, pr, conv_w, conv_b.reshape(1, d_rnn), gate_w, gate_b, lam.reshape(2, 1, d_rnn)]
    if final:
        in_specs += [pl.BlockSpec((1, r, cb), lambda b, j, s: (b, tile_of(s), j)),
                     pl.BlockSpec((1, r, cb), lambda b, j, s: (b, tile_of(s), gate_blk0 + j))]
        args += [yf, pr]
    return pl.pallas_call(
        functools.partial(_lru_kernel, reverse=reverse, n_tiles=n_tiles, final=final),
        grid=(bsz, d_rnn // cb, n_tiles),
        in_specs=in_specs,
        out_specs=pl.BlockSpec((1, r, cb), lambda b, j, s: (b, tile_of(s), j)),
        out_shape=jax.ShapeDtypeStruct((bsz, t, d_rnn), BF16 if final else F32),
        scratch_shapes=[pltpu.VMEM((8, cb), F32), pltpu.VMEM((r + 16, cb), F32),
                        pltpu.VMEM((r + 8, cb), F32), pltpu.VMEM((r + 8, cb), F32)],
        name="lru_scan_bwd" if reverse else "lru_scan_fwd",
        compiler_params=_cparams("parallel", "parallel", "arbitrary"),
    )(*args)


def _even_layer(h2, bsz, t, w_in, conv_w, conv_b, dt_bias, a_log, d_skip, ssd_norm, ret_norm,
                rope_tabs):
    d_model = h2.shape[1]
    d_ssd = d_model
    d_ret = d_model
    gn = SSD_GROUPS * SSD_STATE
    heads = d_ssd // SSD_HEAD_DIM
    qk = RET_HEADS * RET_QK_DIM
    z_w, xs_w, bm_w, cm_w, dt_w, q_w, k_w, v_w, g_w = jnp.split(
        w_in, np.cumsum([d_ssd, d_ssd, gn, gn, 2 * heads, qk, qk, d_ret])[:].tolist(), axis=1)
    w_main = jnp.concatenate([z_w, xs_w, v_w, g_w, q_w, k_w, bm_w, cm_w], axis=1).astype(BF16)
    pr = _matmul(h2, w_main, 512).reshape(bsz, t, -1)
    dt = _matmul(h2, dt_w.astype(BF16), 2 * heads).reshape(bsz, t, 2 * heads)
    z_blk, xs_blk, v_blk, g_blk = 0, 1, 2, 3
    q_off, k_off, bc_off = 4 * d_model, 4 * d_model + qk, 4 * d_model + 2 * qk

    cw = min(512, gn)
    n_xs = d_ssd // cw
    col_map = lambda j: jnp.where(j < n_xs, xs_blk * n_xs + j, bc_off // cw + j - n_xs)
    xbc = _conv_silu(pr, conv_w, conv_b, col_map, d_ssd + 2 * gn, cw)
    q = _rope(pr, q_off // qk, qk, rope_tabs, 1.0)
    k = _rope(pr, k_off // qk, qk, rope_tabs, RET_QK_DIM ** -0.5)

    ysf = _ssd_scan(xbc, dt, dt_bias, a_log, False)
    ysb = _ssd_scan(xbc, dt, dt_bias, a_log, True)
    yrf = _ret_scan(q, k, pr, v_blk, d_ret, False)
    yrb = _ret_scan(q, k, pr, v_blk, d_ret, True)
    d_skip_row = jnp.repeat(d_skip, SSD_HEAD_DIM).reshape(1, d_ssd)
    return _even_out(ysf, ysb, yrf, yrb, xbc, pr, z_blk, g_blk, d_skip_row, ssd_norm, ret_norm)


def _odd_layer(h2, bsz, t, w_in, conv_w, conv_b, gate_w, gate_b, lam):
    d_rnn = conv_w.shape[1]
    pr = _matmul(h2, w_in.astype(BF16), 512).reshape(bsz, t, 2 * d_rnn)
    cb = min(1024, d_rnn)
    gw = gate_w.astype(BF16)
    yf = _lru_scan(pr, d_rnn // cb, conv_w, conv_b, gw, gate_b, lam, False)
    return _lru_scan(pr, d_rnn // cb, conv_w, conv_b, gw, gate_b, lam, True, yf=yf, gate_blk0=0)


def kernel(x, c, ctx, c_ctx, ada_w, ada_b, norm_pre, norm_post, e_w_in, e_conv_w, e_conv_b,
           e_dt_bias, e_a_log, e_d_skip, e_ssd_norm, e_ret_norm, e_w_out, o_w_in, o_conv_w,
           o_conv_b, o_gate_w, o_gate_b, o_lambda, o_w_out):
    bsz, seq, d = x.shape
    depth = ada_w.shape[0]
    t = CTX_LEN + seq
    m = bsz * t
    xs = jnp.concatenate([ctx, x], axis=1).reshape(m, d)

    mod = _modulation(jnp.concatenate([c, c_ctx[None]], axis=0), ada_w, ada_b)
    n_ctx_blk, n_lat_blk = CTX_LEN // ROW_TILE, seq // ROW_TILE
    mod_lat = jnp.broadcast_to(mod[:, :bsz].reshape(depth, bsz, 1, 3, d),
                               (depth, bsz, n_lat_blk, 3, d))
    mod_ctx = jnp.broadcast_to(mod[:, bsz].reshape(depth, 1, 1, 3, d),
                               (depth, bsz, n_ctx_blk, 3, d))
    modblk = jnp.concatenate([mod_ctx, mod_lat], axis=2).reshape(depth, m // ROW_TILE, 3, d)

    rope_tabs = _rope_tables(seq)
    h2 = _prenorm(xs, norm_pre[0], modblk[0])
    for i in range(depth):
        j = i // 2
        if i % 2 == 0:
            y = _even_layer(h2, bsz, t, e_w_in[j], e_conv_w[j], e_conv_b[j], e_dt_bias[j],
                            e_a_log[j], e_d_skip[j], e_ssd_norm[j], e_ret_norm[j], rope_tabs)
            w_out = e_w_out[j]
        else:
            y = _odd_layer(h2, bsz, t, o_w_in[j], o_conv_w[j], o_conv_b[j], o_gate_w[j],
                           o_gate_b[j], o_lambda[j])
            w_out = o_w_out[j]
        out = _matmul(y.reshape(m, -1), w_out.astype(BF16), 512)
        if i + 1 < depth:
            xs, h2 = _resnorm(xs, out, modblk[i], norm_post[i], norm_pre[i + 1], modblk[i + 1])
    return _final_residual(xs.reshape(bsz, t, d), out.reshape(bsz, t, d), modblk[depth - 1],
                           norm_post[depth - 1], seq)
```

```python
import functools
import math

import numpy as np
import jax
import jax.numpy as jnp
from jax import lax
from jax.experimental import pallas as pl
from jax.experimental.pallas import tpu as pltpu

F32 = jnp.float32
BF16 = jnp.bfloat16

GRID_W = 64
CTX_LEN = 256
SSD_HEAD_DIM = 64
SSD_GROUPS = 8
SSD_STATE = 128
RET_HEADS = 16
RET_QK_DIM = 128
RNN_BLOCK = 256
LRU_C = 8.0
CONV_W = 4
CHUNK = 128
ROPE_BASE = 10000.0
EPS = 1e-6

LANES = 128
ROW_TILE = 256
VMEM_LIMIT = 56 * 1024 * 1024
NEG_BIG = -1e30


def _cparams(*sem):
    return pltpu.CompilerParams(dimension_semantics=sem, vmem_limit_bytes=VMEM_LIMIT)


def _silu(x):
    return x * jax.nn.sigmoid(x)


def _softplus(x):
    return jnp.maximum(x, 0.0) + jnp.log1p(jnp.exp(-jnp.abs(x)))


def _mod_kernel(ct_ref, w_ref, b_ref, o_ref, *, n_vec):
    w = w_ref[0]
    o_ref[0] = jnp.zeros(o_ref.shape[1:], F32)
    for m in range(n_vec):
        col = _silu(ct_ref[:, m:m + 1])
        o_ref[0, m:m + 1, :] = jnp.sum(w * col, axis=0, keepdims=True) + b_ref[0]


def _modulation(cvecs, ada_w, ada_b):
    depth, d, n3 = ada_w.shape
    n_vec = cvecs.shape[0]
    ct = jnp.zeros((d, 8), F32).at[:, :n_vec].set(cvecs.T)
    tn = 512
    return pl.pallas_call(
        functools.partial(_mod_kernel, n_vec=n_vec),
        grid=(depth, n3 // tn),
        in_specs=[pl.BlockSpec((d, 8), lambda i, j: (0, 0)),
                  pl.BlockSpec((1, d, tn), lambda i, j: (i, 0, j)),
                  pl.BlockSpec((1, 1, tn), lambda i, j: (i, 0, j))],
        out_specs=pl.BlockSpec((1, 8, tn), lambda i, j: (i, 0, j)),
        out_shape=jax.ShapeDtypeStruct((depth, 8, n3), F32),
        name="modulation",
        compiler_params=_cparams("parallel", "parallel"),
    )(ct, ada_w, ada_b.reshape(depth, 1, n3))


def _prenorm_kernel(x_ref, g_ref, mod_ref, o_ref):
    x = x_ref[...]
    y = x * lax.rsqrt(jnp.mean(x * x, axis=-1, keepdims=True) + EPS) * g_ref[...]
    o_ref[...] = (y * (1.0 + mod_ref[0, 1:2, :]) + mod_ref[0, 0:1, :]).astype(o_ref.dtype)


def _prenorm(x2, g, modblk):
    m, d = x2.shape
    r = ROW_TILE
    return pl.pallas_call(
        _prenorm_kernel,
        grid=(m // r,),
        in_specs=[pl.BlockSpec((r, d), lambda i: (i, 0)),
                  pl.BlockSpec((1, d), lambda i: (0, 0)),
                  pl.BlockSpec((1, 3, d), lambda i: (i, 0, 0))],
        out_specs=pl.BlockSpec((r, d), lambda i: (i, 0)),
        out_shape=jax.ShapeDtypeStruct((m, d), BF16),
        name="prenorm",
        compiler_params=_cparams("parallel"),
    )(x2, g.reshape(1, d), modblk)


def _residual(x, y, gate, g_post):
    return x + gate * (y * lax.rsqrt(jnp.mean(y * y, axis=-1, keepdims=True) + EPS) * g_post)


def _resnorm_kernel(x_ref, y_ref, modp_ref, gpost_ref, gpre_ref, modn_ref, xn_ref, h_ref):
    xn = _residual(x_ref[...], y_ref[...], modp_ref[0, 2:3, :], gpost_ref[...])
    xn_ref[...] = xn
    hn = xn * lax.rsqrt(jnp.mean(xn * xn, axis=-1, keepdims=True) + EPS) * gpre_ref[...]
    h_ref[...] = (hn * (1.0 + modn_ref[0, 1:2, :]) + modn_ref[0, 0:1, :]).astype(h_ref.dtype)


def _resnorm(x2, y2, modp, g_post, g_pre, modn):
    m, d = x2.shape
    r = ROW_TILE
    tile = pl.BlockSpec((r, d), lambda i: (i, 0))
    vec = pl.BlockSpec((1, d), lambda i: (0, 0))
    mod = pl.BlockSpec((1, 3, d), lambda i: (i, 0, 0))
    return pl.pallas_call(
        _resnorm_kernel,
        grid=(m // r,),
        in_specs=[tile, tile, mod, vec, vec, mod],
        out_specs=[tile, tile],
        out_shape=[jax.ShapeDtypeStruct((m, d), F32), jax.ShapeDtypeStruct((m, d), BF16)],
        name="resnorm",
        compiler_params=_cparams("parallel"),
    )(x2, y2, modp, g_post.reshape(1, d), g_pre.reshape(1, d), modn)


def _final_kernel(x_ref, y_ref, modp_ref, gpost_ref, o_ref):
    o_ref[0] = _residual(x_ref[0], y_ref[0], modp_ref[0, 2:3, :], gpost_ref[...])


def _final_residual(x3, y3, modp, g_post, seq):
    bsz, t, d = x3.shape
    r = ROW_TILE
    skip, per = (t - seq) // r, t // r
    tile = pl.BlockSpec((1, r, d), lambda b, i: (b, i + skip, 0))
    return pl.pallas_call(
        _final_kernel,
        grid=(bsz, seq // r),
        in_specs=[tile, tile,
                  pl.BlockSpec((1, 3, d), lambda b, i: (b * per + i + skip, 0, 0)),
                  pl.BlockSpec((1, d), lambda b, i: (0, 0))],
        out_specs=pl.BlockSpec((1, r, d), lambda b, i: (b, i, 0)),
        out_shape=jax.ShapeDtypeStruct((bsz, seq, d), F32),
        name="final_residual",
        compiler_params=_cparams("parallel", "parallel"),
    )(x3, y3, modp, g_post.reshape(1, d))


def _mm_kernel(a_ref, b_ref, o_ref):
    b = b_ref[...].astype(BF16)
    o_ref[...] = jnp.dot(a_ref[...], b, preferred_element_type=F32).astype(o_ref.dtype)


A_TILE_BYTES = 12 * 1024 * 1024


def _matmul(a, b, tn, out_dtype=F32, layer=None):
    m, k = a.shape
    n = b.shape[-1]
    tm = next(c for c in (1536, 1024, 768, 512, 256)
              if m % c == 0 and c * k * a.dtype.itemsize <= A_TILE_BYTES)
    if layer is None:
        b_spec = pl.BlockSpec((k, tn), lambda i, j: (0, j))
    else:
        b_spec = pl.BlockSpec((None, k, tn), lambda i, j: (layer, 0, j))
    return pl.pallas_call(
        _mm_kernel,
        grid=(m // tm, n // tn),
        in_specs=[pl.BlockSpec((tm, k), lambda i, j: (i, 0)), b_spec],
        out_specs=pl.BlockSpec((tm, tn), lambda i, j: (i, j)),
        out_shape=jax.ShapeDtypeStruct((m, n), out_dtype),
        name="proj_k%d" % k,
        compiler_params=_cparams("parallel", "arbitrary"),
    )(a, b)


def _slab_store(ref, row0, val):
    for k in range(ref.shape[0]):
        ref[k, row0:row0 + val.shape[0], :] = val[:, k * LANES:(k + 1) * LANES]


def _slab_load(ref, row0, rows):
    return jnp.concatenate([ref[k, pl.ds(row0, rows, stride=1), :] for k in range(ref.shape[0])],
                           axis=1)


def _conv_taps(x, prev8, next8, w_ref, b_ref, tile, n_tiles, pad_ref):
    r = x.shape[0]
    ctx_tiles = CTX_LEN // r
    prev_ok = jnp.logical_and(tile != 0, tile != ctx_tiles)
    next_ok = jnp.logical_and(tile != ctx_tiles - 1, tile != n_tiles - 1)
    _slab_store(pad_ref, 0, jnp.where(prev_ok, prev8, 0.0))
    _slab_store(pad_ref, 8, x)
    _slab_store(pad_ref, 8 + r, jnp.where(next_ok, next8, 0.0))
    return (b_ref[...] + w_ref[0:1, :] * _slab_load(pad_ref, 6, r)
            + w_ref[1:2, :] * _slab_load(pad_ref, 7, r)
            + w_ref[2:3, :] * x + w_ref[3:4, :] * _slab_load(pad_ref, 9, r))


def _halo_specs(r, cw, col_map, n_tiles):
    rb = r // 8
    last = n_tiles * rb - 1
    return [
        pl.BlockSpec((1, r, cw), lambda b, t, j: (b, t, col_map(j))),
        pl.BlockSpec((1, 8, cw), lambda b, t, j: (b, jnp.maximum(t * rb - 1, 0), col_map(j))),
        pl.BlockSpec((1, 8, cw), lambda b, t, j: (b, jnp.minimum((t + 1) * rb, last), col_map(j))),
    ]


def _conv_silu_kernel(x_ref, p_ref, n_ref, w_ref, b_ref, o_ref, pad_ref, *, n_tiles):
    y = _conv_taps(x_ref[0], p_ref[0], n_ref[0], w_ref, b_ref, pl.program_id(1), n_tiles, pad_ref)
    o_ref[0] = _silu(y).astype(o_ref.dtype)


def _conv_silu(pr, conv_w, conv_b, col_map, n_cols, cw):
    bsz, t, _ = pr.shape
    r = ROW_TILE
    n_tiles = t // r
    return pl.pallas_call(
        functools.partial(_conv_silu_kernel, n_tiles=n_tiles),
        grid=(bsz, n_tiles, n_cols // cw),
        in_specs=_halo_specs(r, cw, col_map, n_tiles) + [
            pl.BlockSpec((CONV_W, cw), lambda b, t, j: (0, j)),
            pl.BlockSpec((1, cw), lambda b, t, j: (0, j))],
        out_specs=pl.BlockSpec((1, r, cw), lambda b, t, j: (b, t, j)),
        out_shape=jax.ShapeDtypeStruct((bsz, t, n_cols), BF16),
        scratch_shapes=[pltpu.VMEM((cw // LANES, r + 16, LANES), F32)],
        name="conv_silu",
        compiler_params=_cparams("parallel", "parallel", "parallel"),
    )(pr, pr, pr, conv_w, conv_b.reshape(1, n_cols))


def _rope_kernel(x_ref, cos_ref, sa_ref, sb_ref, o_ref, *, scale):
    cos, sa, sb = cos_ref[...], sa_ref[...], sb_ref[...]
    q4 = RET_QK_DIM // 4
    for h in range(x_ref.shape[2] // RET_QK_DIM):
        cols = slice(h * RET_QK_DIM, (h + 1) * RET_QK_DIM)
        x = x_ref[0, :, cols]
        y = x * cos + pltpu.roll(x, RET_QK_DIM - q4, 1) * sa + pltpu.roll(x, q4, 1) * sb
        o_ref[0, :, cols] = (y * scale).astype(o_ref.dtype)


def _rope(pr, col_blk, width, tabs, scale):
    bsz, t, _ = pr.shape
    r = ROW_TILE
    tab_spec = pl.BlockSpec((r, RET_QK_DIM), lambda b, i: (i, 0))
    return pl.pallas_call(
        functools.partial(_rope_kernel, scale=scale),
        grid=(bsz, t // r),
        in_specs=[pl.BlockSpec((1, r, width), lambda b, i: (b, i, col_blk)),
                  tab_spec, tab_spec, tab_spec],
        out_specs=pl.BlockSpec((1, r, width), lambda b, i: (b, i, 0)),
        out_shape=jax.ShapeDtypeStruct((bsz, t, width), BF16),
        name="rope",
        compiler_params=_cparams("parallel", "parallel"),
    )(pr, *tabs)


def _rope_tables(seq):
    rows = seq // GRID_W
    row = jnp.repeat(jnp.arange(rows), GRID_W).astype(F32)
    col = jnp.tile(jnp.arange(GRID_W), rows).astype(F32)
    q4 = RET_QK_DIM // 4
    freq = ROPE_BASE ** (-jnp.arange(q4, dtype=F32) / q4)
    ang = jnp.concatenate([row[:, None] * freq, row[:, None] * freq,
                           col[:, None] * freq, col[:, None] * freq], axis=1)
    first = (jnp.arange(RET_QK_DIM) % (2 * q4)) < q4
    cos, sin = jnp.cos(ang), jnp.sin(ang)
    sa = jnp.where(first, -sin, 0.0)
    sb = jnp.where(first, 0.0, sin)
    pad = lambda tab, v: jnp.concatenate([jnp.full((CTX_LEN, RET_QK_DIM), v, F32), tab], axis=0)
    return pad(cos, 1.0), pad(sa, 0.0), pad(sb, 0.0)


def _chunk_order(step, n_ctx, n_all, reverse):
    if not reverse:
        return step
    return jnp.where(step < n_ctx, n_ctx - 1 - step, n_all - 1 + n_ctx - step)


def _tri_mask(reverse):
    i = lax.broadcasted_iota(jnp.int32, (CHUNK, CHUNK), 0)
    j = lax.broadcasted_iota(jnp.int32, (CHUNK, CHUNK), 1)
    return (j >= i) if reverse else (j <= i)


def _ssd_kernel(xbc_ref, dt_ref, bias_ref, alog_ref, y_ref, h_ref, *, reverse, d_ssd, heads):
    hpg = heads // SSD_GROUPS
    gw = hpg * SSD_HEAD_DIM
    d = 1 if reverse else 0

    @pl.when(pl.program_id(1) == 0)
    def _():
        h_ref[...] = jnp.zeros_like(h_ref)

    mask = _tri_mask(reverse)
    hpt = LANES // SSD_HEAD_DIM
    lane_head = lax.broadcasted_iota(jnp.int32, (CHUNK, LANES), 1) // SSD_HEAD_DIM
    lane_head_row = lax.broadcasted_iota(jnp.int32, (1, LANES), 1) // SSD_HEAD_DIM
    lo = d * heads
    dtv = _softplus(dt_ref[0] + bias_ref[...])[:, lo:lo + heads]
    la = dtv * (-jnp.exp(alog_ref[...]))[:, lo:lo + heads]
    acs = jnp.dot(mask.astype(F32), la, precision=lax.Precision.HIGHEST,
                  preferred_element_type=F32)
    tot = acs[0:1, :] if reverse else acs[CHUNK - 1:CHUNK, :]
    e_tot = jnp.exp(tot)
    w = jnp.exp(tot - acs) * dtv
    acs_w_t = jnp.concatenate([acs, w], axis=1).T
    acs_t, w_t = acs_w_t[:heads], acs_w_t[heads:]
    dt_t = jnp.concatenate([dtv, dtv], axis=1).T[:heads]

    bm_off = d_ssd
    cm_off = d_ssd + SSD_GROUPS * SSD_STATE
    for g in range(SSD_GROUPS):
        cg = xbc_ref[0, :, cm_off + g * SSD_STATE:cm_off + (g + 1) * SSD_STATE]
        bg = xbc_ref[0, :, bm_off + g * SSD_STATE:bm_off + (g + 1) * SSD_STATE]
        scores = lax.dot_general(cg, bg, (((1,), (1,)), ((), ())), preferred_element_type=F32)
        cg32 = cg.astype(F32)
        bg_t = bg.astype(F32).T
        hg = h_ref[g]
        hg16 = hg.astype(BF16)
        xg = xbc_ref[0, :, g * gw:(g + 1) * gw]
        for lt in range(gw // LANES):
            cols = slice(lt * LANES, (lt + 1) * LANES)
            xl, hl = xg[:, cols], hg16[:, cols]
            y = upd = dec = None
            for sub in range(hpt):
                hd = g * hpg + lt * hpt + sub
                col_b = jnp.broadcast_to(acs[:, hd:hd + 1], (CHUNK, CHUNK))
                dm = jnp.exp(jnp.where(mask, col_b - acs_t[hd:hd + 1, :], NEG_BIG))
                p = (scores * dm * dt_t[hd:hd + 1, :]).astype(BF16)
                c_in = (cg32 * jnp.exp(col_b)).astype(BF16)
                y_h = (jnp.dot(p, xl, preferred_element_type=F32)
                       + jnp.dot(c_in, hl, preferred_element_type=F32))
                b_out = (bg_t * w_t[hd:hd + 1, :]).astype(BF16)
                upd_h = jnp.dot(b_out, xl, preferred_element_type=F32)
                dec_h = jnp.broadcast_to(e_tot[:, hd:hd + 1], (1, LANES))
                mine = lane_head == sub
                y = y_h if y is None else jnp.where(mine, y_h, y)
                upd = upd_h if upd is None else jnp.where(mine, upd_h, upd)
                dec = dec_h if dec is None else jnp.where(lane_head_row == sub, dec_h, dec)
            y_ref[0, :, g * gw + lt * LANES:g * gw + (lt + 1) * LANES] = y.astype(y_ref.dtype)
            h_ref[g, :, cols] = hg[:, cols] * dec + upd


def _ssd_scan(xbc, dt, dt_bias, a_log, reverse):
    bsz, t, _ = xbc.shape
    heads = dt.shape[2] // 2
    d_ssd = heads * SSD_HEAD_DIM
    nc, n_ctx = t // CHUNK, CTX_LEN // CHUNK
    order = lambda b, s: (b, _chunk_order(s, n_ctx, nc, reverse), 0)
    return pl.pallas_call(
        functools.partial(_ssd_kernel, reverse=reverse, d_ssd=d_ssd, heads=heads),
        grid=(bsz, nc),
        in_specs=[pl.BlockSpec((1, CHUNK, xbc.shape[2]), order),
                  pl.BlockSpec((1, CHUNK, 2 * heads), order),
                  pl.BlockSpec((1, 2 * heads), lambda b, s: (0, 0)),
                  pl.BlockSpec((1, 2 * heads), lambda b, s: (0, 0))],
        out_specs=pl.BlockSpec((1, CHUNK, d_ssd), order),
        out_shape=jax.ShapeDtypeStruct((bsz, t, d_ssd), BF16),
        scratch_shapes=[pltpu.VMEM((SSD_GROUPS, SSD_STATE, d_ssd // SSD_GROUPS), F32)],
        name="ssd_scan_bwd" if reverse else "ssd_scan_fwd",
        compiler_params=_cparams("parallel", "arbitrary"),
    )(xbc, dt, dt_bias.reshape(1, -1), a_log.reshape(1, -1))


def _ret_kernel(q_ref, k_ref, v_ref, y_ref, h_ref, *, reverse, v_dim):
    @pl.when(pl.program_id(1) == 0)
    def _():
        h_ref[...] = jnp.zeros_like(h_ref)

    mask = _tri_mask(reverse)
    i = lax.broadcasted_iota(jnp.int32, (CHUNK, CHUNK), 0)
    j = lax.broadcasted_iota(jnp.int32, (CHUNK, CHUNK), 1)
    dist = (jnp.abs(i - j)).astype(F32)
    n_in = ((CHUNK - i) if reverse else (i + 1)).astype(F32)
    n_out = (i if reverse else (CHUNK - 1 - i)).astype(F32)
    reps = v_dim // CHUNK
    for h in range(RET_HEADS):
        hh = RET_HEADS - 1 - h if reverse else h
        lg = math.log1p(-(2.0 ** (-5.0 - hh)))
        q = q_ref[0, :, h * RET_QK_DIM:(h + 1) * RET_QK_DIM]
        k = k_ref[0, :, h * RET_QK_DIM:(h + 1) * RET_QK_DIM]
        v = v_ref[0, :, h * v_dim:(h + 1) * v_dim].astype(BF16)
        scores = lax.dot_general(q, k, (((1,), (1,)), ((), ())), preferred_element_type=F32)
        p = (scores * jnp.where(mask, jnp.exp(lg * dist), 0.0)).astype(BF16)
        hs = h_ref[h]
        inter = jnp.dot(q, hs.astype(BF16), preferred_element_type=F32)
        e_in = jnp.concatenate([jnp.exp(lg * n_in)] * reps, axis=1)
        y_ref[0, :, h * v_dim:(h + 1) * v_dim] = (
            jnp.dot(p, v, preferred_element_type=F32) + e_in * inter).astype(y_ref.dtype)
        ks = (k.astype(F32) * jnp.exp(lg * n_out)).astype(BF16)
        upd = lax.dot_general(ks, v, (((0,), (0,)), ((), ())), preferred_element_type=F32)
        h_ref[h] = hs * math.exp(lg * CHUNK) + upd


def _ret_scan(q, k, pr, v_blk, v_width, reverse):
    bsz, t, qk_width = q.shape
    v_dim = v_width // RET_HEADS
    nc, n_ctx = t // CHUNK, CTX_LEN // CHUNK
    order = lambda b, s: (b, _chunk_order(s, n_ctx, nc, reverse), 0)
    return pl.pallas_call(
        functools.partial(_ret_kernel, reverse=reverse, v_dim=v_dim),
        grid=(bsz, nc),
        in_specs=[pl.BlockSpec((1, CHUNK, qk_width), order),
                  pl.BlockSpec((1, CHUNK, qk_width), order),
                  pl.BlockSpec((1, CHUNK, v_width),
                               lambda b, s: (b, _chunk_order(s, n_ctx, nc, reverse), v_blk))],
        out_specs=pl.BlockSpec((1, CHUNK, v_width), order),
        out_shape=jax.ShapeDtypeStruct((bsz, t, v_width), BF16),
        scratch_shapes=[pltpu.VMEM((RET_HEADS, RET_QK_DIM, v_dim), F32)],
        name="ret_scan_bwd" if reverse else "ret_scan_fwd",
        compiler_params=_cparams("parallel", "arbitrary"),
    )(q, k, pr)


def _even_out_kernel(ysf_ref, ysb_ref, yrf_ref, yrb_ref, xs_ref, z_ref, g_ref,
                     dsk_ref, sn_ref, rn_ref, o_ref, *, d_ssd, v_dim):
    ys = (ysf_ref[0].astype(F32) + ysb_ref[0].astype(F32)
          + dsk_ref[...] * xs_ref[0].astype(F32))
    ys = ys * _silu(z_ref[0])
    ys = ys * lax.rsqrt(jnp.mean(ys * ys, axis=-1, keepdims=True) + EPS) * sn_ref[...]
    o_ref[0, :, :d_ssd] = ys.astype(o_ref.dtype)
    for h in range(RET_HEADS):
        cols = slice(h * v_dim, (h + 1) * v_dim)
        yr = yrf_ref[0, :, cols].astype(F32) + yrb_ref[0, :, cols].astype(F32)
        mu = jnp.mean(yr, axis=-1, keepdims=True)
        dev = yr - mu
        var = jnp.mean(dev * dev, axis=-1, keepdims=True)
        yn = dev * lax.rsqrt(var + EPS) * rn_ref[:, cols]
        o_ref[0, :, d_ssd + h * v_dim:d_ssd + (h + 1) * v_dim] = (
            yn * _silu(g_ref[0, :, cols])).astype(o_ref.dtype)


def _even_out(ysf, ysb, yrf, yrb, xbc, pa, pb, z_blk, g_blk, d_skip_row, ssd_norm, ret_norm):
    bsz, t, d_ssd = ysf.shape
    d_ret = yrf.shape[2]
    r = CHUNK
    blk = lambda w, c: pl.BlockSpec((1, r, w), lambda b, i: (b, i, c))
    row = lambda w: pl.BlockSpec((1, w), lambda b, i: (0, 0))
    return pl.pallas_call(
        functools.partial(_even_out_kernel, d_ssd=d_ssd, v_dim=d_ret // RET_HEADS),
        grid=(bsz, t // r),
        in_specs=[blk(d_ssd, 0), blk(d_ssd, 0), blk(d_ret, 0), blk(d_ret, 0),
                  blk(d_ssd, 0), blk(d_ssd, z_blk), blk(d_ret, g_blk),
                  row(d_ssd), row(d_ssd), row(d_ret)],
        out_specs=blk(d_ssd + d_ret, 0),
        out_shape=jax.ShapeDtypeStruct((bsz, t, d_ssd + d_ret), BF16),
        name="even_out",
        compiler_params=_cparams("parallel", "parallel"),
    )(ysf, ysb, yrf, yrb, xbc, pa, pb, d_skip_row, ssd_norm.reshape(1, -1),
      ret_norm.reshape(1, -1))


def _lru_kernel(*refs, reverse, n_tiles, final):
    if final:
        (x_ref, p_ref, n_ref, cw_ref, cb_ref, gw_ref, gb_ref, lam_ref, yf_ref, gate_ref,
         o_ref, carry_ref, pad_ref, as_ref, us_ref) = refs
    else:
        (x_ref, p_ref, n_ref, cw_ref, cb_ref, gw_ref, gb_ref, lam_ref,
         o_ref, carry_ref, pad_ref, as_ref, us_ref) = refs
    step = pl.program_id(2)
    tile = _chunk_order(step, CTX_LEN // ROW_TILE, n_tiles, reverse)

    @pl.when(step == 0)
    def _():
        carry_ref[...] = jnp.zeros_like(carry_ref)

    xf = _conv_taps(x_ref[0], p_ref[0], n_ref[0], cw_ref, cb_ref, tile, n_tiles, pad_ref)
    r_rows, cb = xf.shape
    xb = xf.astype(BF16)
    neg_c_sp = -LRU_C * _softplus(-lam_ref[0])
    a_parts, u_parts = [], []
    for kb in range(cb // RNN_BLOCK):
        cols = slice(kb * RNN_BLOCK, (kb + 1) * RNN_BLOCK)
        xk = xb[:, cols]
        rg = jax.nn.sigmoid(jnp.dot(xk, gw_ref[0, 0, kb], preferred_element_type=F32)
                            + gb_ref[0, 0:1, cols])
        ig = jax.nn.sigmoid(jnp.dot(xk, gw_ref[0, 1, kb], preferred_element_type=F32)
                            + gb_ref[0, 1:2, cols])
        a = jnp.exp(neg_c_sp[:, cols] * rg)
        a_parts.append(a)
        s = 1.0 - a * a
        root = s * lax.rsqrt(jnp.maximum(s, jnp.finfo(F32).tiny))
        u_parts.append(root * (ig * xf[:, cols]))
    a = jnp.concatenate(a_parts, axis=1)
    u = jnp.concatenate(u_parts, axis=1)

    body = 0 if reverse else 8
    pad = r_rows if reverse else 0
    _slab_store(as_ref, pad, jnp.ones((8, cb), F32))
    _slab_store(us_ref, pad, jnp.zeros((8, cb), F32))
    for s in (1, 2, 4):
        _slab_store(as_ref, body, a)
        _slab_store(us_ref, body, u)
        src = body + s if reverse else body - s
        u = a * _slab_load(us_ref, src, r_rows) + u
        a = a * _slab_load(as_ref, src, r_rows)
    h = carry_ref[...]
    n_grp = r_rows // 8
    outs = [None] * n_grp
    for v in (range(n_grp - 1, -1, -1) if reverse else range(n_grp)):
        h = a[v * 8:(v + 1) * 8] * h + u[v * 8:(v + 1) * 8]
        outs[v] = h
    last = h[0:1] if reverse else h[7:8]
    carry_ref[...] = jnp.broadcast_to(last, carry_ref.shape)
    y = jnp.concatenate(outs, axis=0)
    if final:
        o_ref[0] = ((yf_ref[0] + y) * _silu(gate_ref[0])).astype(o_ref.dtype)
    else:
        o_ref[0] = y


def _lru_scan(pr, x_blk0, conv_w, conv_b, gate_w, gate_b, lam, reverse, yf=None, gate_blk0=0):
    bsz, t, _ = pr.shape
    d_rnn = conv_w.shape[1]
    cb = min(1024, d_rnn)
    r = ROW_TILE
    n_tiles = t // r
    d = 1 if reverse else 0
    final = yf is not None
    tile_of = lambda s: _chunk_order(s, CTX_LEN // r, n_tiles, reverse)
    rb = r // 8
    last = n_tiles * rb - 1
    in_specs = [
        pl.BlockSpec((1, r, cb), lambda b, j, s: (b, tile_of(s), x_blk0 + j)),
        pl.BlockSpec((1, 8, cb), lambda b, j, s: (b, jnp.maximum(tile_of(s) * rb - 1, 0), x_blk0 + j)),
        pl.BlockSpec((1, 8, cb), lambda b, j, s: (b, jnp.minimum((tile_of(s) + 1) * rb, last), x_blk0 + j)),
        pl.BlockSpec((CONV_W, cb), lambda b, j, s: (0, j)),
        pl.BlockSpec((1, cb), lambda b, j, s: (0, j)),
        pl.BlockSpec((1, 2, cb // RNN_BLOCK, RNN_BLOCK, RNN_BLOCK), lambda b, j, s: (d, 0, j, 0, 0)),
        pl.BlockSpec((1, 2, cb), lambda b, j, s: (d, 0, j)),
        pl.BlockSpec((1, 1, cb), lambda b, j, s: (d, 0, j)),
    ]
    args = [pr, pr, pr, conv_w, conv_b.reshape(1, d_rnn), gate_w, gate_b, lam.reshape(2, 1, d_rnn)]
    if final:
        in_specs += [pl.BlockSpec((1, r, cb), lambda b, j, s: (b, tile_of(s), j)),
                     pl.BlockSpec((1, r, cb), lambda b, j, s: (b, tile_of(s), gate_blk0 + j))]
        args += [yf, pr]
    return pl.pallas_call(
        functools.partial(_lru_kernel, reverse=reverse, n_tiles=n_tiles, final=final),
        grid=(bsz, d_rnn // cb, n_tiles),
        in_specs=in_specs,
        out_specs=pl.BlockSpec((1, r, cb), lambda b, j, s: (b, tile_of(s), j)),
        out_shape=jax.ShapeDtypeStruct((bsz, t, d_rnn), BF16 if final else F32),
        scratch_shapes=[pltpu.VMEM((8, cb), F32), pltpu.VMEM((cb // LANES, r + 16, LANES), F32),
                        pltpu.VMEM((cb // LANES, r + 8, LANES), F32),
                        pltpu.VMEM((cb // LANES, r + 8, LANES), F32)],
        name="lru_scan_bwd" if reverse else "lru_scan_fwd",
        compiler_params=_cparams("parallel", "parallel", "arbitrary"),
    )(*args)


def _even_layer(h2, bsz, t, w_in, conv_w, conv_b, dt_bias, a_log, d_skip, ssd_norm, ret_norm,
                rope_tabs):
    d_model = h2.shape[1]
    d_ssd = d_model
    d_ret = d_model
    gn = SSD_GROUPS * SSD_STATE
    heads = d_ssd // SSD_HEAD_DIM
    qk = RET_HEADS * RET_QK_DIM
    n_a = 2 * d_ssd + 2 * gn
    pa = _matmul(h2, w_in[:, :n_a].astype(BF16), 512).reshape(bsz, t, -1)
    dt = _matmul(h2, w_in[:, n_a:n_a + 2 * heads].astype(BF16), 2 * heads).reshape(bsz, t, -1)
    pb = _matmul(h2, w_in[:, n_a + 2 * heads:].astype(BF16), 512).reshape(bsz, t, -1)
    z_blk = 0
    q_blk, k_blk = 0, 1
    v_blk, g_blk = 2 * qk // d_ret, 2 * qk // d_ret + 1

    cw = min(512, gn)
    xbc = _conv_silu(pa, conv_w, conv_b, lambda j: d_ssd // cw + j, d_ssd + 2 * gn, cw)
    q = _rope(pb, q_blk, qk, rope_tabs, 1.0)
    k = _rope(pb, k_blk, qk, rope_tabs, RET_QK_DIM ** -0.5)

    ysf = _ssd_scan(xbc, dt, dt_bias, a_log, False)
    ysb = _ssd_scan(xbc, dt, dt_bias, a_log, True)
    yrf = _ret_scan(q, k, pb, v_blk, d_ret, False)
    yrb = _ret_scan(q, k, pb, v_blk, d_ret, True)
    d_skip_row = jnp.repeat(d_skip, SSD_HEAD_DIM).reshape(1, d_ssd)
    return _even_out(ysf, ysb, yrf, yrb, xbc, pa, pb, z_blk, g_blk, d_skip_row, ssd_norm,
                     ret_norm)


def _odd_layer(h2, bsz, t, w_in_all, layer, conv_w, conv_b, gate_w, gate_b, lam):
    d_rnn = conv_w.shape[1]
    pr = _matmul(h2, w_in_all, 512, layer=layer).reshape(bsz, t, 2 * d_rnn)
    cb = min(1024, d_rnn)
    gw = gate_w.astype(BF16)
    yf = _lru_scan(pr, d_rnn // cb, conv_w, conv_b, gw, gate_b, lam, False)
    return _lru_scan(pr, d_rnn // cb, conv_w, conv_b, gw, gate_b, lam, True, yf=yf, gate_blk0=0)


def kernel(x, c, ctx, c_ctx, ada_w, ada_b, norm_pre, norm_post, e_w_in, e_conv_w, e_conv_b,
           e_dt_bias, e_a_log, e_d_skip, e_ssd_norm, e_ret_norm, e_w_out, o_w_in, o_conv_w,
           o_conv_b, o_gate_w, o_gate_b, o_lambda, o_w_out):
    bsz, seq, d = x.shape
    depth = ada_w.shape[0]
    t = CTX_LEN + seq
    m = bsz * t
    xs = jnp.concatenate([ctx, x], axis=1).reshape(m, d)

    mod = _modulation(jnp.concatenate([c, c_ctx[None]], axis=0), ada_w, ada_b)
    n_ctx_blk, n_lat_blk = CTX_LEN // ROW_TILE, seq // ROW_TILE
    mod_lat = jnp.broadcast_to(mod[:, :bsz].reshape(depth, bsz, 1, 3, d),
                               (depth, bsz, n_lat_blk, 3, d))
    mod_ctx = jnp.broadcast_to(mod[:, bsz].reshape(depth, 1, 1, 3, d),
                               (depth, bsz, n_ctx_blk, 3, d))
    modblk = jnp.concatenate([mod_ctx, mod_lat], axis=2).reshape(depth, m // ROW_TILE, 3, d)

    rope_tabs = _rope_tables(seq)
    h2 = _prenorm(xs, norm_pre[0], modblk[0])
    for i in range(depth):
        j = i // 2
        if i % 2 == 0:
            y = _even_layer(h2, bsz, t, e_w_in[j], e_conv_w[j], e_conv_b[j], e_dt_bias[j],
                            e_a_log[j], e_d_skip[j], e_ssd_norm[j], e_ret_norm[j], rope_tabs)
            w_out = e_w_out[j]
        else:
            y = _odd_layer(h2, bsz, t, o_w_in, j, o_conv_w[j], o_conv_b[j], o_gate_w[j],
                           o_gate_b[j], o_lambda[j])
            w_out = o_w_out[j]
        out = _matmul(y.reshape(m, -1), w_out.astype(BF16), 512)
        if i + 1 < depth:
            xs, h2 = _resnorm(xs, out, modblk[i], norm_post[i], norm_pre[i + 1], modblk[i + 1])
    return _final_residual(xs.reshape(bsz, t, d), out.reshape(bsz, t, d), modblk[depth - 1],
                           norm_post[depth - 1], seq)
```

```python
import functools
import math

import numpy as np
import jax
import jax.numpy as jnp
from jax import lax
from jax.experimental import pallas as pl
from jax.experimental.pallas import tpu as pltpu

F32 = jnp.float32
BF16 = jnp.bfloat16

GRID_W = 64
CTX_LEN = 256
SSD_HEAD_DIM = 64
SSD_GROUPS = 8
SSD_STATE = 128
RET_HEADS = 16
RET_QK_DIM = 128
RNN_BLOCK = 256
LRU_C = 8.0
CONV_W = 4
CHUNK = 128
ROPE_BASE = 10000.0
EPS = 1e-6

LANES = 128
ROW_TILE = 256
COL_TILE = 2048
VMEM_LIMIT = 56 * 1024 * 1024
NEG_BIG = -1e30


def _cparams(*sem):
    return pltpu.CompilerParams(dimension_semantics=sem, vmem_limit_bytes=VMEM_LIMIT)


def _silu(x):
    return x * jax.nn.sigmoid(x)


def _softplus(x):
    return jnp.maximum(x, 0.0) + jnp.log1p(jnp.exp(-jnp.abs(x)))


def _mod_kernel(ct_ref, w_ref, b_ref, o_ref, *, n_vec):
    w = w_ref[0]
    o_ref[0] = jnp.zeros(o_ref.shape[1:], F32)
    for m in range(n_vec):
        col = _silu(ct_ref[:, m:m + 1])
        o_ref[0, m:m + 1, :] = jnp.sum(w * col, axis=0, keepdims=True) + b_ref[0]


def _modulation(cvecs, ada_w, ada_b):
    depth, d, n3 = ada_w.shape
    n_vec = cvecs.shape[0]
    ct = jnp.zeros((d, 8), F32).at[:, :n_vec].set(cvecs.T)
    tn = 512
    return pl.pallas_call(
        functools.partial(_mod_kernel, n_vec=n_vec),
        grid=(depth, n3 // tn),
        in_specs=[pl.BlockSpec((d, 8), lambda i, j: (0, 0)),
                  pl.BlockSpec((1, d, tn), lambda i, j: (i, 0, j)),
                  pl.BlockSpec((1, 1, tn), lambda i, j: (i, 0, j))],
        out_specs=pl.BlockSpec((1, 8, tn), lambda i, j: (i, 0, j)),
        out_shape=jax.ShapeDtypeStruct((depth, 8, n3), F32),
        name="modulation",
        compiler_params=_cparams("parallel", "parallel"),
    )(ct, ada_w, ada_b.reshape(depth, 1, n3))


def _prenorm_kernel(x_ref, g_ref, mod_ref, o_ref):
    x = x_ref[...]
    y = x * lax.rsqrt(jnp.mean(x * x, axis=-1, keepdims=True) + EPS) * g_ref[...]
    o_ref[...] = (y * (1.0 + mod_ref[0, 1:2, :]) + mod_ref[0, 0:1, :]).astype(o_ref.dtype)


def _prenorm(x2, g, modblk):
    m, d = x2.shape
    r = ROW_TILE
    return pl.pallas_call(
        _prenorm_kernel,
        grid=(m // r,),
        in_specs=[pl.BlockSpec((r, d), lambda i: (i, 0)),
                  pl.BlockSpec((1, d), lambda i: (0, 0)),
                  pl.BlockSpec((1, 3, d), lambda i: (i, 0, 0))],
        out_specs=pl.BlockSpec((r, d), lambda i: (i, 0)),
        out_shape=jax.ShapeDtypeStruct((m, d), BF16),
        name="prenorm",
        compiler_params=_cparams("parallel"),
    )(x2, g.reshape(1, d), modblk)


def _residual(x, y, gate, g_post):
    return x + gate * (y * lax.rsqrt(jnp.mean(y * y, axis=-1, keepdims=True) + EPS) * g_post)


def _resnorm_kernel(x_ref, y_ref, modp_ref, gpost_ref, gpre_ref, modn_ref, xn_ref, h_ref):
    xn = _residual(x_ref[...], y_ref[...], modp_ref[0, 2:3, :], gpost_ref[...])
    xn_ref[...] = xn
    hn = xn * lax.rsqrt(jnp.mean(xn * xn, axis=-1, keepdims=True) + EPS) * gpre_ref[...]
    h_ref[...] = (hn * (1.0 + modn_ref[0, 1:2, :]) + modn_ref[0, 0:1, :]).astype(h_ref.dtype)


def _resnorm(x2, y2, modp, g_post, g_pre, modn):
    m, d = x2.shape
    r = ROW_TILE
    tile = pl.BlockSpec((r, d), lambda i: (i, 0))
    vec = pl.BlockSpec((1, d), lambda i: (0, 0))
    mod = pl.BlockSpec((1, 3, d), lambda i: (i, 0, 0))
    return pl.pallas_call(
        _resnorm_kernel,
        grid=(m // r,),
        in_specs=[tile, tile, mod, vec, vec, mod],
        out_specs=[tile, tile],
        out_shape=[jax.ShapeDtypeStruct((m, d), F32), jax.ShapeDtypeStruct((m, d), BF16)],
        name="resnorm",
        compiler_params=_cparams("parallel"),
    )(x2, y2, modp, g_post.reshape(1, d), g_pre.reshape(1, d), modn)


def _final_kernel(x_ref, y_ref, modp_ref, gpost_ref, o_ref):
    o_ref[0] = _residual(x_ref[0], y_ref[0], modp_ref[0, 2:3, :], gpost_ref[...])


def _final_residual(x3, y3, modp, g_post, seq):
    bsz, t, d = x3.shape
    r = ROW_TILE
    skip, per = (t - seq) // r, t // r
    tile = pl.BlockSpec((1, r, d), lambda b, i: (b, i + skip, 0))
    return pl.pallas_call(
        _final_kernel,
        grid=(bsz, seq // r),
        in_specs=[tile, tile,
                  pl.BlockSpec((1, 3, d), lambda b, i: (b * per + i + skip, 0, 0)),
                  pl.BlockSpec((1, d), lambda b, i: (0, 0))],
        out_specs=pl.BlockSpec((1, r, d), lambda b, i: (b, i, 0)),
        out_shape=jax.ShapeDtypeStruct((bsz, seq, d), F32),
        name="final_residual",
        compiler_params=_cparams("parallel", "parallel"),
    )(x3, y3, modp, g_post.reshape(1, d))


def _mm_kernel(a_ref, b_ref, o_ref):
    b = b_ref[...].astype(BF16)
    o_ref[...] = jnp.dot(a_ref[...], b, preferred_element_type=F32).astype(o_ref.dtype)


A_TILE_BYTES = 12 * 1024 * 1024


def _mm_stack_kernel(a_ref, b_ref, o_ref):
    b = b_ref[0].astype(BF16)
    o_ref[...] = jnp.dot(a_ref[...], b, preferred_element_type=F32).astype(o_ref.dtype)


def _matmul(a, b, tn, out_dtype=F32, layer=None, col0=0, n=None):
    m, k = a.shape
    n = b.shape[-1] if n is None else n
    tm = next(c for c in (1536, 1024, 768, 512, 256)
              if m % c == 0 and c * k * a.dtype.itemsize <= A_TILE_BYTES)
    if layer is not None and (col0 % LANES or tn % LANES):
        b, layer = b[layer, :, col0:col0 + n].astype(BF16), None
    if layer is None:
        body = _mm_kernel
        b_spec = pl.BlockSpec((k, tn), lambda i, j: (0, j))
    else:
        body = _mm_stack_kernel
        b_spec = pl.BlockSpec(
            (pl.Element(1), pl.Element(k), pl.Element(tn)),
            lambda i, j: (layer, 0, pl.multiple_of(col0 + j * tn, LANES)))
    return pl.pallas_call(
        body,
        grid=(m // tm, n // tn),
        in_specs=[pl.BlockSpec((tm, k), lambda i, j: (i, 0)), b_spec],
        out_specs=pl.BlockSpec((tm, tn), lambda i, j: (i, j)),
        out_shape=jax.ShapeDtypeStruct((m, n), out_dtype),
        name="proj_k%d" % k,
        compiler_params=_cparams("parallel", "arbitrary"),
    )(a, b)


def _slab_store(ref, row0, val):
    for k in range(ref.shape[0]):
        ref[k, row0:row0 + val.shape[0], :] = val[:, k * LANES:(k + 1) * LANES]


def _slab_load(ref, row0, rows):
    return jnp.concatenate([ref[k, pl.ds(row0, rows, stride=1), :] for k in range(ref.shape[0])],
                           axis=1)


def _conv_taps(x, prev8, next8, w_ref, b_ref, tile, n_tiles, pad_ref):
    r = x.shape[0]
    ctx_tiles = CTX_LEN // r
    prev_ok = jnp.logical_and(tile != 0, tile != ctx_tiles)
    next_ok = jnp.logical_and(tile != ctx_tiles - 1, tile != n_tiles - 1)
    _slab_store(pad_ref, 0, jnp.where(prev_ok, prev8, 0.0))
    _slab_store(pad_ref, 8, x)
    _slab_store(pad_ref, 8 + r, jnp.where(next_ok, next8, 0.0))
    return (b_ref[...] + w_ref[0:1, :] * _slab_load(pad_ref, 6, r)
            + w_ref[1:2, :] * _slab_load(pad_ref, 7, r)
            + w_ref[2:3, :] * x + w_ref[3:4, :] * _slab_load(pad_ref, 9, r))


def _halo_specs(r, cw, col_map, n_tiles):
    rb = r // 8
    last = n_tiles * rb - 1
    return [
        pl.BlockSpec((1, r, cw), lambda b, t, j: (b, t, col_map(j))),
        pl.BlockSpec((1, 8, cw), lambda b, t, j: (b, jnp.maximum(t * rb - 1, 0), col_map(j))),
        pl.BlockSpec((1, 8, cw), lambda b, t, j: (b, jnp.minimum((t + 1) * rb, last), col_map(j))),
    ]


def _conv_silu_kernel(x_ref, p_ref, n_ref, w_ref, b_ref, o_ref, pad_ref, *, n_tiles):
    y = _conv_taps(x_ref[0], p_ref[0], n_ref[0], w_ref, b_ref, pl.program_id(1), n_tiles, pad_ref)
    o_ref[0] = _silu(y).astype(o_ref.dtype)


def _conv_silu(pr, conv_w, conv_b, col_map, n_cols, cw):
    bsz, t, _ = pr.shape
    r = ROW_TILE
    n_tiles = t // r
    return pl.pallas_call(
        functools.partial(_conv_silu_kernel, n_tiles=n_tiles),
        grid=(bsz, n_tiles, n_cols // cw),
        in_specs=_halo_specs(r, cw, col_map, n_tiles) + [
            pl.BlockSpec((CONV_W, cw), lambda b, t, j: (0, j)),
            pl.BlockSpec((1, cw), lambda b, t, j: (0, j))],
        out_specs=pl.BlockSpec((1, r, cw), lambda b, t, j: (b, t, j)),
        out_shape=jax.ShapeDtypeStruct((bsz, t, n_cols), BF16),
        scratch_shapes=[pltpu.VMEM((cw // LANES, r + 16, LANES), F32)],
        name="conv_silu",
        compiler_params=_cparams("parallel", "parallel", "parallel"),
    )(pr, pr, pr, conv_w, conv_b.reshape(1, n_cols))


def _rope_kernel(x_ref, cos_ref, sa_ref, sb_ref, o_ref, *, scale):
    cos, sa, sb = cos_ref[...], sa_ref[...], sb_ref[...]
    q4 = RET_QK_DIM // 4
    for h in range(x_ref.shape[2] // RET_QK_DIM):
        cols = slice(h * RET_QK_DIM, (h + 1) * RET_QK_DIM)
        x = x_ref[0, :, cols]
        y = x * cos + pltpu.roll(x, RET_QK_DIM - q4, 1) * sa + pltpu.roll(x, q4, 1) * sb
        o_ref[0, :, cols] = (y * scale).astype(o_ref.dtype)


def _rope(pr, col_blk, width, tabs, scale):
    bsz, t, _ = pr.shape
    r = ROW_TILE
    tab_spec = pl.BlockSpec((r, RET_QK_DIM), lambda b, i: (i, 0))
    return pl.pallas_call(
        functools.partial(_rope_kernel, scale=scale),
        grid=(bsz, t // r),
        in_specs=[pl.BlockSpec((1, r, width), lambda b, i: (b, i, col_blk)),
                  tab_spec, tab_spec, tab_spec],
        out_specs=pl.BlockSpec((1, r, width), lambda b, i: (b, i, 0)),
        out_shape=jax.ShapeDtypeStruct((bsz, t, width), BF16),
        name="rope",
        compiler_params=_cparams("parallel", "parallel"),
    )(pr, *tabs)


def _rope_tables(seq):
    rows = seq // GRID_W
    row = jnp.repeat(jnp.arange(rows), GRID_W).astype(F32)
    col = jnp.tile(jnp.arange(GRID_W), rows).astype(F32)
    q4 = RET_QK_DIM // 4
    freq = ROPE_BASE ** (-jnp.arange(q4, dtype=F32) / q4)
    ang = jnp.concatenate([row[:, None] * freq, row[:, None] * freq,
                           col[:, None] * freq, col[:, None] * freq], axis=1)
    first = (jnp.arange(RET_QK_DIM) % (2 * q4)) < q4
    cos, sin = jnp.cos(ang), jnp.sin(ang)
    sa = jnp.where(first, -sin, 0.0)
    sb = jnp.where(first, 0.0, sin)
    pad = lambda tab, v: jnp.concatenate([jnp.full((CTX_LEN, RET_QK_DIM), v, F32), tab], axis=0)
    return pad(cos, 1.0), pad(sa, 0.0), pad(sb, 0.0)


def _chunk_order(step, n_ctx, n_all, reverse):
    if not reverse:
        return step
    return jnp.where(step < n_ctx, n_ctx - 1 - step, n_all - 1 + n_ctx - step)


def _tri_mask(reverse):
    i = lax.broadcasted_iota(jnp.int32, (CHUNK, CHUNK), 0)
    j = lax.broadcasted_iota(jnp.int32, (CHUNK, CHUNK), 1)
    return (j >= i) if reverse else (j <= i)


def _ssd_kernel(xbc_ref, dt_ref, bias_ref, alog_ref, y_ref, h_ref, *, reverse, d_ssd, heads):
    hpg = heads // SSD_GROUPS
    gw = hpg * SSD_HEAD_DIM
    d = 1 if reverse else 0

    @pl.when(pl.program_id(1) == 0)
    def _():
        h_ref[...] = jnp.zeros_like(h_ref)

    mask = _tri_mask(reverse)
    hpt = LANES // SSD_HEAD_DIM
    lane_head = lax.broadcasted_iota(jnp.int32, (CHUNK, LANES), 1) // SSD_HEAD_DIM
    lane_head_row = lax.broadcasted_iota(jnp.int32, (1, LANES), 1) // SSD_HEAD_DIM
    lo = d * heads
    dtv = _softplus(dt_ref[0] + bias_ref[...])[:, lo:lo + heads]
    la = dtv * (-jnp.exp(alog_ref[...]))[:, lo:lo + heads]
    acs = jnp.dot(mask.astype(F32), la, precision=lax.Precision.HIGHEST,
                  preferred_element_type=F32)
    tot = acs[0:1, :] if reverse else acs[CHUNK - 1:CHUNK, :]
    e_tot = jnp.exp(tot)
    w = jnp.exp(tot - acs) * dtv
    acs_w_t = jnp.concatenate([acs, w], axis=1).T
    acs_t, w_t = acs_w_t[:heads], acs_w_t[heads:]
    dt_t = jnp.concatenate([dtv, dtv], axis=1).T[:heads]

    bm_off = d_ssd
    cm_off = d_ssd + SSD_GROUPS * SSD_STATE
    for g in range(SSD_GROUPS):
        cg = xbc_ref[0, :, cm_off + g * SSD_STATE:cm_off + (g + 1) * SSD_STATE]
        bg = xbc_ref[0, :, bm_off + g * SSD_STATE:bm_off + (g + 1) * SSD_STATE]
        scores = lax.dot_general(cg, bg, (((1,), (1,)), ((), ())), preferred_element_type=F32)
        cg32 = cg.astype(F32)
        bg_t = bg.astype(F32).T
        hg = h_ref[g]
        hg16 = hg.astype(BF16)
        xg = xbc_ref[0, :, g * gw:(g + 1) * gw]
        for lt in range(gw // LANES):
            cols = slice(lt * LANES, (lt + 1) * LANES)
            xl, hl = xg[:, cols], hg16[:, cols]
            y = upd = dec = None
            for sub in range(hpt):
                hd = g * hpg + lt * hpt + sub
                col_b = jnp.broadcast_to(acs[:, hd:hd + 1], (CHUNK, CHUNK))
                dm = jnp.exp(jnp.where(mask, col_b - acs_t[hd:hd + 1, :], NEG_BIG))
                p = (scores * dm * dt_t[hd:hd + 1, :]).astype(BF16)
                c_in = (cg32 * jnp.exp(col_b)).astype(BF16)
                y_h = (jnp.dot(p, xl, preferred_element_type=F32)
                       + jnp.dot(c_in, hl, preferred_element_type=F32))
                b_out = (bg_t * w_t[hd:hd + 1, :]).astype(BF16)
                upd_h = jnp.dot(b_out, xl, preferred_element_type=F32)
                dec_h = jnp.broadcast_to(e_tot[:, hd:hd + 1], (1, LANES))
                mine = lane_head == sub
                y = y_h if y is None else jnp.where(mine, y_h, y)
                upd = upd_h if upd is None else jnp.where(mine, upd_h, upd)
                dec = dec_h if dec is None else jnp.where(lane_head_row == sub, dec_h, dec)
            y_ref[0, :, g * gw + lt * LANES:g * gw + (lt + 1) * LANES] = y.astype(y_ref.dtype)
            h_ref[g, :, cols] = hg[:, cols] * dec + upd


def _ssd_scan(xbc, dt, dt_bias, a_log, reverse):
    bsz, t, _ = xbc.shape
    heads = dt.shape[2] // 2
    d_ssd = heads * SSD_HEAD_DIM
    nc, n_ctx = t // CHUNK, CTX_LEN // CHUNK
    order = lambda b, s: (b, _chunk_order(s, n_ctx, nc, reverse), 0)
    return pl.pallas_call(
        functools.partial(_ssd_kernel, reverse=reverse, d_ssd=d_ssd, heads=heads),
        grid=(bsz, nc),
        in_specs=[pl.BlockSpec((1, CHUNK, xbc.shape[2]), order),
                  pl.BlockSpec((1, CHUNK, 2 * heads), order),
                  pl.BlockSpec((1, 2 * heads), lambda b, s: (0, 0)),
                  pl.BlockSpec((1, 2 * heads), lambda b, s: (0, 0))],
        out_specs=pl.BlockSpec((1, CHUNK, d_ssd), order),
        out_shape=jax.ShapeDtypeStruct((bsz, t, d_ssd), BF16),
        scratch_shapes=[pltpu.VMEM((SSD_GROUPS, SSD_STATE, d_ssd // SSD_GROUPS), F32)],
        name="ssd_scan_bwd" if reverse else "ssd_scan_fwd",
        compiler_params=_cparams("parallel", "arbitrary"),
    )(xbc, dt, dt_bias.reshape(1, -1), a_log.reshape(1, -1))


def _ret_kernel(q_ref, k_ref, v_ref, y_ref, h_ref, *, reverse, v_dim):
    @pl.when(pl.program_id(1) == 0)
    def _():
        h_ref[...] = jnp.zeros_like(h_ref)

    mask = _tri_mask(reverse)
    i = lax.broadcasted_iota(jnp.int32, (CHUNK, CHUNK), 0)
    j = lax.broadcasted_iota(jnp.int32, (CHUNK, CHUNK), 1)
    dist = (jnp.abs(i - j)).astype(F32)
    n_in = ((CHUNK - i) if reverse else (i + 1)).astype(F32)
    n_out = (i if reverse else (CHUNK - 1 - i)).astype(F32)
    reps = v_dim // CHUNK
    for h in range(RET_HEADS):
        hh = RET_HEADS - 1 - h if reverse else h
        lg = math.log1p(-(2.0 ** (-5.0 - hh)))
        q = q_ref[0, :, h * RET_QK_DIM:(h + 1) * RET_QK_DIM]
        k = k_ref[0, :, h * RET_QK_DIM:(h + 1) * RET_QK_DIM]
        v = v_ref[0, :, h * v_dim:(h + 1) * v_dim].astype(BF16)
        scores = lax.dot_general(q, k, (((1,), (1,)), ((), ())), preferred_element_type=F32)
        p = (scores * jnp.where(mask, jnp.exp(lg * dist), 0.0)).astype(BF16)
        hs = h_ref[h]
        inter = jnp.dot(q, hs.astype(BF16), preferred_element_type=F32)
        e_in = jnp.concatenate([jnp.exp(lg * n_in)] * reps, axis=1)
        y_ref[0, :, h * v_dim:(h + 1) * v_dim] = (
            jnp.dot(p, v, preferred_element_type=F32) + e_in * inter).astype(y_ref.dtype)
        ks = (k.astype(F32) * jnp.exp(lg * n_out)).astype(BF16)
        upd = lax.dot_general(ks, v, (((0,), (0,)), ((), ())), preferred_element_type=F32)
        h_ref[h] = hs * math.exp(lg * CHUNK) + upd


def _ret_scan(q, k, pr, v_blk, v_width, reverse):
    bsz, t, qk_width = q.shape
    v_dim = v_width // RET_HEADS
    nc, n_ctx = t // CHUNK, CTX_LEN // CHUNK
    order = lambda b, s: (b, _chunk_order(s, n_ctx, nc, reverse), 0)
    return pl.pallas_call(
        functools.partial(_ret_kernel, reverse=reverse, v_dim=v_dim),
        grid=(bsz, nc),
        in_specs=[pl.BlockSpec((1, CHUNK, qk_width), order),
                  pl.BlockSpec((1, CHUNK, qk_width), order),
                  pl.BlockSpec((1, CHUNK, v_width),
                               lambda b, s: (b, _chunk_order(s, n_ctx, nc, reverse), v_blk))],
        out_specs=pl.BlockSpec((1, CHUNK, v_width), order),
        out_shape=jax.ShapeDtypeStruct((bsz, t, v_width), BF16),
        scratch_shapes=[pltpu.VMEM((RET_HEADS, RET_QK_DIM, v_dim), F32)],
        name="ret_scan_bwd" if reverse else "ret_scan_fwd",
        compiler_params=_cparams("parallel", "arbitrary"),
    )(q, k, pr)


def _even_out_kernel(ysf_ref, ysb_ref, yrf_ref, yrb_ref, xs_ref, z_ref, g_ref,
                     dsk_ref, sn_ref, rn_ref, o_ref, *, d_ssd, v_dim):
    ys = (ysf_ref[0].astype(F32) + ysb_ref[0].astype(F32)
          + dsk_ref[...] * xs_ref[0].astype(F32))
    ys = ys * _silu(z_ref[0])
    ys = ys * lax.rsqrt(jnp.mean(ys * ys, axis=-1, keepdims=True) + EPS) * sn_ref[...]
    o_ref[0, :, :d_ssd] = ys.astype(o_ref.dtype)
    for h in range(RET_HEADS):
        cols = slice(h * v_dim, (h + 1) * v_dim)
        yr = yrf_ref[0, :, cols].astype(F32) + yrb_ref[0, :, cols].astype(F32)
        mu = jnp.mean(yr, axis=-1, keepdims=True)
        dev = yr - mu
        var = jnp.mean(dev * dev, axis=-1, keepdims=True)
        yn = dev * lax.rsqrt(var + EPS) * rn_ref[:, cols]
        o_ref[0, :, d_ssd + h * v_dim:d_ssd + (h + 1) * v_dim] = (
            yn * _silu(g_ref[0, :, cols])).astype(o_ref.dtype)


def _even_out(ysf, ysb, yrf, yrb, xbc, pa, pb, z_blk, g_blk, d_skip_row, ssd_norm, ret_norm):
    bsz, t, d_ssd = ysf.shape
    d_ret = yrf.shape[2]
    r = CHUNK
    blk = lambda w, c: pl.BlockSpec((1, r, w), lambda b, i: (b, i, c))
    row = lambda w: pl.BlockSpec((1, w), lambda b, i: (0, 0))
    return pl.pallas_call(
        functools.partial(_even_out_kernel, d_ssd=d_ssd, v_dim=d_ret // RET_HEADS),
        grid=(bsz, t // r),
        in_specs=[blk(d_ssd, 0), blk(d_ssd, 0), blk(d_ret, 0), blk(d_ret, 0),
                  blk(d_ssd, 0), blk(d_ssd, z_blk), blk(d_ret, g_blk),
                  row(d_ssd), row(d_ssd), row(d_ret)],
        out_specs=blk(d_ssd + d_ret, 0),
        out_shape=jax.ShapeDtypeStruct((bsz, t, d_ssd + d_ret), BF16),
        name="even_out",
        compiler_params=_cparams("parallel", "parallel"),
    )(ysf, ysb, yrf, yrb, xbc, pa, pb, d_skip_row, ssd_norm.reshape(1, -1),
      ret_norm.reshape(1, -1))


def _lru_kernel(*refs, reverse, n_tiles, final):
    if final:
        (x_ref, p_ref, n_ref, cw_ref, cb_ref, gw_ref, gb_ref, lam_ref, yf_ref, gate_ref,
         o_ref, carry_ref, pad_ref, as_ref, us_ref) = refs
    else:
        (x_ref, p_ref, n_ref, cw_ref, cb_ref, gw_ref, gb_ref, lam_ref,
         o_ref, carry_ref, pad_ref, as_ref, us_ref) = refs
    step = pl.program_id(2)
    tile = _chunk_order(step, CTX_LEN // ROW_TILE, n_tiles, reverse)

    @pl.when(step == 0)
    def _():
        carry_ref[...] = jnp.zeros_like(carry_ref)

    xf = _conv_taps(x_ref[0], p_ref[0], n_ref[0], cw_ref, cb_ref, tile, n_tiles, pad_ref)
    r_rows, cb = xf.shape
    xb = xf.astype(BF16)
    rate = (-LRU_C / math.log(2.0)) * _softplus(-lam_ref[0])
    a_parts, u_parts = [], []
    for kb in range(cb // RNN_BLOCK):
        cols = slice(kb * RNN_BLOCK, (kb + 1) * RNN_BLOCK)
        xk = xb[:, cols]
        rg = jax.nn.sigmoid(jnp.dot(xk, gw_ref[0, 0, kb], preferred_element_type=F32)
                            + gb_ref[0, 0:1, cols])
        ig = jax.nn.sigmoid(jnp.dot(xk, gw_ref[0, 1, kb], preferred_element_type=F32)
                            + gb_ref[0, 1:2, cols])
        a = jnp.exp2(rate[:, cols] * rg)
        a_parts.append(a)
        s = 1.0 - a * a
        root = s * lax.rsqrt(jnp.maximum(s, jnp.finfo(F32).tiny))
        u_parts.append(root * (ig * xf[:, cols]))
    a = jnp.concatenate(a_parts, axis=1)
    u = jnp.concatenate(u_parts, axis=1)

    body = 0 if reverse else 8
    pad = r_rows if reverse else 0
    _slab_store(as_ref, pad, jnp.ones((8, cb), F32))
    _slab_store(us_ref, pad, jnp.zeros((8, cb), F32))
    for s in (1, 2, 4):
        _slab_store(as_ref, body, a)
        _slab_store(us_ref, body, u)
        src = body + s if reverse else body - s
        u = a * _slab_load(us_ref, src, r_rows) + u
        a = a * _slab_load(as_ref, src, r_rows)
    h = carry_ref[...]
    n_grp = r_rows // 8
    outs = [None] * n_grp
    for v in (range(n_grp - 1, -1, -1) if reverse else range(n_grp)):
        h = a[v * 8:(v + 1) * 8] * h + u[v * 8:(v + 1) * 8]
        outs[v] = h
    last = h[0:1] if reverse else h[7:8]
    carry_ref[...] = jnp.broadcast_to(last, carry_ref.shape)
    y = jnp.concatenate(outs, axis=0)
    if final:
        o_ref[0] = ((yf_ref[0] + y) * _silu(gate_ref[0])).astype(o_ref.dtype)
    else:
        o_ref[0] = y


def _lru_scan(pr, x_blk0, conv_w, conv_b, gate_w, gate_b, lam, reverse, yf=None, gate_blk0=0):
    bsz, t, _ = pr.shape
    d_rnn = conv_w.shape[1]
    cb = min(COL_TILE, d_rnn)
    r = ROW_TILE
    n_tiles = t // r
    d = 1 if reverse else 0
    final = yf is not None
    tile_of = lambda s: _chunk_order(s, CTX_LEN // r, n_tiles, reverse)
    rb = r // 8
    last = n_tiles * rb - 1
    in_specs = [
        pl.BlockSpec((1, r, cb), lambda b, j, s: (b, tile_of(s), x_blk0 + j)),
        pl.BlockSpec((1, 8, cb), lambda b, j, s: (b, jnp.maximum(tile_of(s) * rb - 1, 0), x_blk0 + j)),
        pl.BlockSpec((1, 8, cb), lambda b, j, s: (b, jnp.minimum((tile_of(s) + 1) * rb, last), x_blk0 + j)),
        pl.BlockSpec((CONV_W, cb), lambda b, j, s: (0, j)),
        pl.BlockSpec((1, cb), lambda b, j, s: (0, j)),
        pl.BlockSpec((1, 2, cb // RNN_BLOCK, RNN_BLOCK, RNN_BLOCK), lambda b, j, s: (d, 0, j, 0, 0)),
        pl.BlockSpec((1, 2, cb), lambda b, j, s: (d, 0, j)),
        pl.BlockSpec((1, 1, cb), lambda b, j, s: (d, 0, j)),
    ]
    args = [pr, pr, pr, conv_w, conv_b.reshape(1, d_rnn), gate_w, gate_b, lam.reshape(2, 1, d_rnn)]
    if final:
        in_specs += [pl.BlockSpec((1, r, cb), lambda b, j, s: (b, tile_of(s), j)),
                     pl.BlockSpec((1, r, cb), lambda b, j, s: (b, tile_of(s), gate_blk0 + j))]
        args += [yf, pr]
    return pl.pallas_call(
        functools.partial(_lru_kernel, reverse=reverse, n_tiles=n_tiles, final=final),
        grid=(bsz, d_rnn // cb, n_tiles),
        in_specs=in_specs,
        out_specs=pl.BlockSpec((1, r, cb), lambda b, j, s: (b, tile_of(s), j)),
        out_shape=jax.ShapeDtypeStruct((bsz, t, d_rnn), BF16 if final else F32),
        scratch_shapes=[pltpu.VMEM((8, cb), F32), pltpu.VMEM((cb // LANES, r + 16, LANES), F32),
                        pltpu.VMEM((cb // LANES, r + 8, LANES), F32),
                        pltpu.VMEM((cb // LANES, r + 8, LANES), F32)],
        name="lru_scan_bwd" if reverse else "lru_scan_fwd",
        compiler_params=_cparams("parallel", "parallel", "arbitrary"),
    )(*args)


def _even_layer(h2, bsz, t, w_in_all, layer, conv_w, conv_b, dt_bias, a_log, d_skip, ssd_norm,
                ret_norm, rope_tabs):
    d_model = h2.shape[1]
    d_ssd = d_model
    d_ret = d_model
    gn = SSD_GROUPS * SSD_STATE
    heads = d_ssd // SSD_HEAD_DIM
    qk = RET_HEADS * RET_QK_DIM
    n_a, n_dt = 2 * d_ssd + 2 * gn, 2 * heads
    n_b = w_in_all.shape[2] - n_a - n_dt
    pa = _matmul(h2, w_in_all, 512, layer=layer, col0=0, n=n_a).reshape(bsz, t, -1)
    pb = _matmul(h2, w_in_all, 512, layer=layer, col0=n_a + n_dt, n=n_b).reshape(bsz, t, -1)
    dt = _matmul(h2, w_in_all, n_dt, layer=layer, col0=n_a, n=n_dt).reshape(bsz, t, n_dt)
    z_blk = 0
    q_blk, k_blk = 0, 1
    v_blk, g_blk = 2 * qk // d_ret, 2 * qk // d_ret + 1

    cw = math.gcd(math.gcd(d_ssd, 2 * gn), COL_TILE)
    xbc = _conv_silu(pa, conv_w, conv_b, lambda j: d_ssd // cw + j, d_ssd + 2 * gn, cw)
    q = _rope(pb, q_blk, qk, rope_tabs, 1.0)
    k = _rope(pb, k_blk, qk, rope_tabs, RET_QK_DIM ** -0.5)

    ysf = _ssd_scan(xbc, dt, dt_bias, a_log, False)
    ysb = _ssd_scan(xbc, dt, dt_bias, a_log, True)
    yrf = _ret_scan(q, k, pb, v_blk, d_ret, False)
    yrb = _ret_scan(q, k, pb, v_blk, d_ret, True)
    d_skip_row = jnp.repeat(d_skip, SSD_HEAD_DIM).reshape(1, d_ssd)
    return _even_out(ysf, ysb, yrf, yrb, xbc, pa, pb, z_blk, g_blk, d_skip_row, ssd_norm,
                     ret_norm)


def _odd_layer(h2, bsz, t, w_in_all, layer, conv_w, conv_b, gate_w, gate_b, lam):
    d_rnn = conv_w.shape[1]
    pr = _matmul(h2, w_in_all, 512, layer=layer).reshape(bsz, t, 2 * d_rnn)
    cb = min(COL_TILE, d_rnn)
    gw = gate_w.astype(BF16)
    yf = _lru_scan(pr, d_rnn // cb, conv_w, conv_b, gw, gate_b, lam, False)
    return _lru_scan(pr, d_rnn // cb, conv_w, conv_b, gw, gate_b, lam, True, yf=yf, gate_blk0=0)


def kernel(x, c, ctx, c_ctx, ada_w, ada_b, norm_pre, norm_post, e_w_in, e_conv_w, e_conv_b,
           e_dt_bias, e_a_log, e_d_skip, e_ssd_norm, e_ret_norm, e_w_out, o_w_in, o_conv_w,
           o_conv_b, o_gate_w, o_gate_b, o_lambda, o_w_out):
    bsz, seq, d = x.shape
    depth = ada_w.shape[0]
    t = CTX_LEN + seq
    m = bsz * t
    xs = jnp.concatenate([ctx, x], axis=1).reshape(m, d)

    mod = _modulation(jnp.concatenate([c, c_ctx[None]], axis=0), ada_w, ada_b)
    n_ctx_blk, n_lat_blk = CTX_LEN // ROW_TILE, seq // ROW_TILE
    mod_lat = jnp.broadcast_to(mod[:, :bsz].reshape(depth, bsz, 1, 3, d),
                               (depth, bsz, n_lat_blk, 3, d))
    mod_ctx = jnp.broadcast_to(mod[:, bsz].reshape(depth, 1, 1, 3, d),
                               (depth, bsz, n_ctx_blk, 3, d))
    modblk = jnp.concatenate([mod_ctx, mod_lat], axis=2).reshape(depth, m // ROW_TILE, 3, d)

    rope_tabs = _rope_tables(seq)
    h2 = _prenorm(xs, norm_pre[0], modblk[0])
    for i in range(depth):
        j = i // 2
        if i % 2 == 0:
            y = _even_layer(h2, bsz, t, e_w_in, j, e_conv_w[j], e_conv_b[j], e_dt_bias[j],
                            e_a_log[j], e_d_skip[j], e_ssd_norm[j], e_ret_norm[j], rope_tabs)
            w_out = e_w_out[j]
        else:
            y = _odd_layer(h2, bsz, t, o_w_in, j, o_conv_w[j], o_conv_b[j], o_gate_w[j],
                           o_gate_b[j], o_lambda[j])
            w_out = o_w_out[j]
        out = _matmul(y.reshape(m, -1), w_out.astype(BF16), 512)
        if i + 1 < depth:
            xs, h2 = _resnorm(xs, out, modblk[i], norm_post[i], norm_pre[i + 1], modblk[i + 1])
    return _final_residual(xs.reshape(bsz, t, d), out.reshape(bsz, t, d), modblk[depth - 1],
                           norm_post[depth - 1], seq)
```

```python
import functools
import math

import numpy as np
import jax
import jax.numpy as jnp
from jax import lax
from jax.experimental import pallas as pl
from jax.experimental.pallas import tpu as pltpu

F32 = jnp.float32
BF16 = jnp.bfloat16

GRID_W = 64
CTX_LEN = 256
SSD_HEAD_DIM = 64
SSD_GROUPS = 8
SSD_STATE = 128
RET_HEADS = 16
RET_QK_DIM = 128
RNN_BLOCK = 256
LRU_C = 8.0
CONV_W = 4
CHUNK = 128
ROPE_BASE = 10000.0
EPS = 1e-6

LANES = 128
MXU_COLS = 256
ROW_TILE = 256
COL_TILE = 2048
VMEM_LIMIT = 56 * 1024 * 1024
NEG_BIG = -1e30


def _cparams(*sem):
    return pltpu.CompilerParams(dimension_semantics=sem, vmem_limit_bytes=VMEM_LIMIT)


def _silu(x):
    return x * jax.nn.sigmoid(x)


def _softplus(x):
    return jnp.maximum(x, 0.0) + jnp.log1p(jnp.exp(-jnp.abs(x)))


def _mod_kernel(ct_ref, w_ref, b_ref, o_ref, *, n_vec):
    w = w_ref[0]
    o_ref[0] = jnp.zeros(o_ref.shape[1:], F32)
    for m in range(n_vec):
        col = _silu(ct_ref[:, m:m + 1])
        o_ref[0, m:m + 1, :] = jnp.sum(w * col, axis=0, keepdims=True) + b_ref[0]


def _modulation(cvecs, ada_w, ada_b):
    depth, d, n3 = ada_w.shape
    n_vec = cvecs.shape[0]
    ct = jnp.zeros((d, 8), F32).at[:, :n_vec].set(cvecs.T)
    tn = 512
    return pl.pallas_call(
        functools.partial(_mod_kernel, n_vec=n_vec),
        grid=(depth, n3 // tn),
        in_specs=[pl.BlockSpec((d, 8), lambda i, j: (0, 0)),
                  pl.BlockSpec((1, d, tn), lambda i, j: (i, 0, j)),
                  pl.BlockSpec((1, 1, tn), lambda i, j: (i, 0, j))],
        out_specs=pl.BlockSpec((1, 8, tn), lambda i, j: (i, 0, j)),
        out_shape=jax.ShapeDtypeStruct((depth, 8, n3), F32),
        name="modulation",
        compiler_params=_cparams("parallel", "parallel"),
    )(ct, ada_w, ada_b.reshape(depth, 1, n3))


def _prenorm_kernel(ctx_ref, x_ref, g_ref, mod_ref, xs_ref, h_ref, *, ctx_tiles):
    x = jnp.where(pl.program_id(1) < ctx_tiles, ctx_ref[0], x_ref[0])
    xs_ref[0] = x
    y = x * lax.rsqrt(jnp.mean(x * x, axis=-1, keepdims=True) + EPS) * g_ref[...]
    h_ref[0] = (y * (1.0 + mod_ref[0, 1:2, :]) + mod_ref[0, 0:1, :]).astype(h_ref.dtype)


def _prenorm(ctx, x, g, modblk):
    bsz, seq, d = x.shape
    r = ROW_TILE
    ctx_tiles = ctx.shape[1] // r
    n_tiles = ctx_tiles + seq // r
    tile = pl.BlockSpec((1, r, d), lambda b, i: (b, i, 0))
    return pl.pallas_call(
        functools.partial(_prenorm_kernel, ctx_tiles=ctx_tiles),
        grid=(bsz, n_tiles),
        in_specs=[pl.BlockSpec((1, r, d), lambda b, i: (b, jnp.minimum(i, ctx_tiles - 1), 0)),
                  pl.BlockSpec((1, r, d), lambda b, i: (b, jnp.maximum(i - ctx_tiles, 0), 0)),
                  pl.BlockSpec((1, d), lambda b, i: (0, 0)),
                  pl.BlockSpec((1, 3, d), lambda b, i: (b * n_tiles + i, 0, 0))],
        out_specs=[tile, tile],
        out_shape=[jax.ShapeDtypeStruct((bsz, n_tiles * r, d), F32),
                   jax.ShapeDtypeStruct((bsz, n_tiles * r, d), BF16)],
        name="prenorm",
        compiler_params=_cparams("parallel", "parallel"),
    )(ctx, x, g.reshape(1, d), modblk)


def _residual(x, y, gate, g_post):
    return x + gate * (y * lax.rsqrt(jnp.mean(y * y, axis=-1, keepdims=True) + EPS) * g_post)


def _resnorm_kernel(x_ref, y_ref, modp_ref, gpost_ref, gpre_ref, modn_ref, xn_ref, h_ref):
    xn = _residual(x_ref[...], y_ref[...], modp_ref[0, 2:3, :], gpost_ref[...])
    xn_ref[...] = xn
    hn = xn * lax.rsqrt(jnp.mean(xn * xn, axis=-1, keepdims=True) + EPS) * gpre_ref[...]
    h_ref[...] = (hn * (1.0 + modn_ref[0, 1:2, :]) + modn_ref[0, 0:1, :]).astype(h_ref.dtype)


def _resnorm(x2, y2, modp, g_post, g_pre, modn):
    m, d = x2.shape
    r = ROW_TILE
    tile = pl.BlockSpec((r, d), lambda i: (i, 0))
    vec = pl.BlockSpec((1, d), lambda i: (0, 0))
    mod = pl.BlockSpec((1, 3, d), lambda i: (i, 0, 0))
    return pl.pallas_call(
        _resnorm_kernel,
        grid=(m // r,),
        in_specs=[tile, tile, mod, vec, vec, mod],
        out_specs=[tile, tile],
        out_shape=[jax.ShapeDtypeStruct((m, d), F32), jax.ShapeDtypeStruct((m, d), BF16)],
        name="resnorm",
        compiler_params=_cparams("parallel"),
    )(x2, y2, modp, g_post.reshape(1, d), g_pre.reshape(1, d), modn)


def _final_kernel(x_ref, y_ref, modp_ref, gpost_ref, o_ref):
    o_ref[0] = _residual(x_ref[0], y_ref[0], modp_ref[0, 2:3, :], gpost_ref[...])


def _final_residual(x3, y3, modp, g_post, seq):
    bsz, t, d = x3.shape
    r = ROW_TILE
    skip, per = (t - seq) // r, t // r
    tile = pl.BlockSpec((1, r, d), lambda b, i: (b, i + skip, 0))
    return pl.pallas_call(
        _final_kernel,
        grid=(bsz, seq // r),
        in_specs=[tile, tile,
                  pl.BlockSpec((1, 3, d), lambda b, i: (b * per + i + skip, 0, 0)),
                  pl.BlockSpec((1, d), lambda b, i: (0, 0))],
        out_specs=pl.BlockSpec((1, r, d), lambda b, i: (b, i, 0)),
        out_shape=jax.ShapeDtypeStruct((bsz, seq, d), F32),
        name="final_residual",
        compiler_params=_cparams("parallel", "parallel"),
    )(x3, y3, modp, g_post.reshape(1, d))


A_TILE_BYTES = 12 * 1024 * 1024


def _mm_tile(a_ref, b_ref, cols=slice(None)):
    b = b_ref[0, :, cols] if len(b_ref.shape) == 3 else b_ref[:, cols]
    return jnp.dot(a_ref[...], b.astype(BF16), preferred_element_type=F32)


def _mm_kernel(a_ref, b_ref, o_ref):
    o_ref[...] = _mm_tile(a_ref, b_ref).astype(o_ref.dtype)


def _mm_rope_kernel(a_ref, b_ref, cos_ref, sin_ref, qk_ref, o_ref, *, q_tiles, qk_tiles,
                    k_scale):
    j = pl.program_id(1)

    @pl.when(j < qk_tiles)
    def _():
        scale = jnp.where(j < q_tiles, 1.0, k_scale)
        cos, sin = cos_ref[...], sin_ref[...]
        q4 = RET_QK_DIM // 4
        lane = lax.broadcasted_iota(jnp.int32, cos.shape, 1)
        first = (lane % (2 * q4)) < q4
        for pair in range(qk_ref.shape[1] // MXU_COLS):
            acc = _mm_tile(a_ref, b_ref, slice(pair * MXU_COLS, (pair + 1) * MXU_COLS))
            for h in range(MXU_COLS // RET_QK_DIM):
                x = acc[:, h * RET_QK_DIM:(h + 1) * RET_QK_DIM]
                partner = jnp.where(first, pltpu.roll(x, RET_QK_DIM - q4, 1), pltpu.roll(x, q4, 1))
                y = x * cos + partner * sin
                c0 = pair * MXU_COLS + h * RET_QK_DIM
                qk_ref[:, c0:c0 + RET_QK_DIM] = (y * scale).astype(qk_ref.dtype)

    @pl.when(j >= qk_tiles)
    def _():
        o_ref[...] = _mm_tile(a_ref, b_ref)


def _proj_operands(a, b, tn, layer, col0, n):
    m, k = a.shape
    tm = next(c for c in (1536, 1024, 768, 512, 256)
              if m % c == 0 and c * k * a.dtype.itemsize <= A_TILE_BYTES)
    if layer is not None and (col0 % LANES or tn % LANES):
        b, layer = b[layer, :, col0:col0 + n].astype(BF16), None
    if layer is None:
        b_spec = pl.BlockSpec((k, tn), lambda i, j: (0, j))
    else:
        b_spec = pl.BlockSpec(
            (pl.Element(1), pl.Element(k), pl.Element(tn)),
            lambda i, j: (layer, 0, pl.multiple_of(col0 + j * tn, LANES)))
    return tm, b, b_spec


def _matmul(a, b, tn, out_dtype=F32, layer=None, col0=0, n=None):
    m, k = a.shape
    n = b.shape[-1] if n is None else n
    tm, b, b_spec = _proj_operands(a, b, tn, layer, col0, n)
    return pl.pallas_call(
        _mm_kernel,
        grid=(m // tm, n // tn),
        in_specs=[pl.BlockSpec((tm, k), lambda i, j: (i, 0)), b_spec],
        out_specs=pl.BlockSpec((tm, tn), lambda i, j: (i, j)),
        out_shape=jax.ShapeDtypeStruct((m, n), out_dtype),
        name="proj_k%d" % k,
        compiler_params=_cparams("parallel", "arbitrary"),
    )(a, b)


def _matmul_rope(a, b, tn, layer, col0, n_qk, n_rest, tabs, k_scale):
    m, k = a.shape
    tm, b, b_spec = _proj_operands(a, b, tn, layer, col0, n_qk + n_rest)
    qk_tiles = n_qk // tn
    tab_spec = pl.BlockSpec((tm, RET_QK_DIM), lambda i, j: (i, 0))
    return pl.pallas_call(
        functools.partial(_mm_rope_kernel, q_tiles=qk_tiles // 2, qk_tiles=qk_tiles,
                          k_scale=k_scale),
        grid=(m // tm, (n_qk + n_rest) // tn),
        in_specs=[pl.BlockSpec((tm, k), lambda i, j: (i, 0)), b_spec, tab_spec, tab_spec],
        out_specs=[pl.BlockSpec((tm, tn), lambda i, j: (i, jnp.minimum(j, qk_tiles - 1))),
                   pl.BlockSpec((tm, tn), lambda i, j: (i, jnp.maximum(j - qk_tiles, 0)))],
        out_shape=[jax.ShapeDtypeStruct((m, n_qk), BF16),
                   jax.ShapeDtypeStruct((m, n_rest), F32)],
        name="proj_rope_k%d" % k,
        compiler_params=_cparams("parallel", "arbitrary"),
    )(a, b, *tabs)


def _slab_store(ref, row0, val):
    for k in range(ref.shape[0]):
        ref[k, row0:row0 + val.shape[0], :] = val[:, k * LANES:(k + 1) * LANES]


def _slab_load(ref, row0, rows):
    return jnp.concatenate([ref[k, pl.ds(row0, rows, stride=1), :] for k in range(ref.shape[0])],
                           axis=1)


def _conv_taps(x, prev8, next8, w_ref, b_ref, tile, n_tiles, pad_ref):
    r = x.shape[0]
    ctx_tiles = CTX_LEN // r
    prev_ok = jnp.logical_and(tile != 0, tile != ctx_tiles)
    next_ok = jnp.logical_and(tile != ctx_tiles - 1, tile != n_tiles - 1)
    _slab_store(pad_ref, 0, jnp.where(prev_ok, prev8, 0.0))
    _slab_store(pad_ref, 8, x)
    _slab_store(pad_ref, 8 + r, jnp.where(next_ok, next8, 0.0))
    return (b_ref[...] + w_ref[0:1, :] * _slab_load(pad_ref, 6, r)
            + w_ref[1:2, :] * _slab_load(pad_ref, 7, r)
            + w_ref[2:3, :] * x + w_ref[3:4, :] * _slab_load(pad_ref, 9, r))


def _halo_specs(r, cw, col_map, n_tiles):
    rb = r // 8
    last = n_tiles * rb - 1
    return [
        pl.BlockSpec((1, r, cw), lambda b, t, j: (b, t, col_map(j))),
        pl.BlockSpec((1, 8, cw), lambda b, t, j: (b, jnp.maximum(t * rb - 1, 0), col_map(j))),
        pl.BlockSpec((1, 8, cw), lambda b, t, j: (b, jnp.minimum((t + 1) * rb, last), col_map(j))),
    ]


def _conv_silu_kernel(x_ref, p_ref, n_ref, w_ref, b_ref, o_ref, pad_ref, *, n_tiles):
    y = _conv_taps(x_ref[0], p_ref[0], n_ref[0], w_ref, b_ref, pl.program_id(1), n_tiles, pad_ref)
    o_ref[0] = _silu(y).astype(o_ref.dtype)


def _conv_silu(pr, conv_w, conv_b, col_map, n_cols, cw):
    bsz, t, _ = pr.shape
    r = ROW_TILE
    n_tiles = t // r
    return pl.pallas_call(
        functools.partial(_conv_silu_kernel, n_tiles=n_tiles),
        grid=(bsz, n_tiles, n_cols // cw),
        in_specs=_halo_specs(r, cw, col_map, n_tiles) + [
            pl.BlockSpec((CONV_W, cw), lambda b, t, j: (0, j)),
            pl.BlockSpec((1, cw), lambda b, t, j: (0, j))],
        out_specs=pl.BlockSpec((1, r, cw), lambda b, t, j: (b, t, j)),
        out_shape=jax.ShapeDtypeStruct((bsz, t, n_cols), BF16),
        scratch_shapes=[pltpu.VMEM((cw // LANES, r + 16, LANES), F32)],
        name="conv_silu",
        compiler_params=_cparams("parallel", "parallel", "parallel"),
    )(pr, pr, pr, conv_w, conv_b.reshape(1, n_cols))


def _rope_tables(seq, bsz):
    rows = seq // GRID_W
    row = jnp.repeat(jnp.arange(rows), GRID_W).astype(F32)
    col = jnp.tile(jnp.arange(GRID_W), rows).astype(F32)
    q4 = RET_QK_DIM // 4
    freq = ROPE_BASE ** (-jnp.arange(q4, dtype=F32) / q4)
    ang = jnp.concatenate([row[:, None] * freq, row[:, None] * freq,
                           col[:, None] * freq, col[:, None] * freq], axis=1)
    first = (jnp.arange(RET_QK_DIM) % (2 * q4)) < q4
    cos, sin = jnp.cos(ang), jnp.sin(ang)
    sin = jnp.where(first, -sin, sin)
    pad = lambda tab, v: jnp.tile(
        jnp.concatenate([jnp.full((CTX_LEN, RET_QK_DIM), v, F32), tab], axis=0), (bsz, 1))
    return pad(cos, 1.0), pad(sin, 0.0)


def _chunk_order(step, n_ctx, n_all, reverse):
    if not reverse:
        return step
    return jnp.where(step < n_ctx, n_ctx - 1 - step, n_all - 1 + n_ctx - step)


def _tri_mask(reverse):
    i = lax.broadcasted_iota(jnp.int32, (CHUNK, CHUNK), 0)
    j = lax.broadcasted_iota(jnp.int32, (CHUNK, CHUNK), 1)
    return (j >= i) if reverse else (j <= i)


def _ssd_kernel(xbc_ref, dt_ref, bias_ref, alog_ref, y_ref, h_ref, *, reverse, d_ssd, heads):
    hpg = heads // SSD_GROUPS
    gw = hpg * SSD_HEAD_DIM
    d = 1 if reverse else 0

    @pl.when(pl.program_id(1) == 0)
    def _():
        h_ref[...] = jnp.zeros_like(h_ref)

    mask = _tri_mask(reverse)
    hpt = LANES // SSD_HEAD_DIM
    lane_head = lax.broadcasted_iota(jnp.int32, (CHUNK, LANES), 1) // SSD_HEAD_DIM
    lane_head_row = lax.broadcasted_iota(jnp.int32, (1, LANES), 1) // SSD_HEAD_DIM
    lo = d * heads
    dtv = _softplus(dt_ref[0] + bias_ref[...])[:, lo:lo + heads]
    la = dtv * (-jnp.exp(alog_ref[...]))[:, lo:lo + heads]
    acs = jnp.dot(mask.astype(F32), la, precision=lax.Precision.HIGHEST,
                  preferred_element_type=F32)
    tot = acs[0:1, :] if reverse else acs[CHUNK - 1:CHUNK, :]
    e_tot = jnp.exp(tot)
    w = jnp.exp(tot - acs) * dtv
    acs_w_t = jnp.concatenate([acs, w], axis=1).T
    acs_t, w_t = acs_w_t[:heads], acs_w_t[heads:]
    dt_t = jnp.concatenate([dtv, dtv], axis=1).T[:heads]

    bm_off = d_ssd
    cm_off = d_ssd + SSD_GROUPS * SSD_STATE
    for g in range(SSD_GROUPS):
        cg = xbc_ref[0, :, cm_off + g * SSD_STATE:cm_off + (g + 1) * SSD_STATE]
        bg = xbc_ref[0, :, bm_off + g * SSD_STATE:bm_off + (g + 1) * SSD_STATE]
        scores = lax.dot_general(cg, bg, (((1,), (1,)), ((), ())), preferred_element_type=F32)
        cg32 = cg.astype(F32)
        bg_t = bg.astype(F32).T
        hg = h_ref[g]
        hg16 = hg.astype(BF16)
        xg = xbc_ref[0, :, g * gw:(g + 1) * gw]
        for lt in range(gw // LANES):
            cols = slice(lt * LANES, (lt + 1) * LANES)
            xl, hl = xg[:, cols], hg16[:, cols]
            y = upd = dec = None
            for sub in range(hpt):
                hd = g * hpg + lt * hpt + sub
                col_b = jnp.broadcast_to(acs[:, hd:hd + 1], (CHUNK, CHUNK))
                dm = jnp.exp(jnp.where(mask, col_b - acs_t[hd:hd + 1, :], NEG_BIG))
                p = (scores * dm * dt_t[hd:hd + 1, :]).astype(BF16)
                c_in = (cg32 * jnp.exp(col_b)).astype(BF16)
                y_h = (jnp.dot(p, xl, preferred_element_type=F32)
                       + jnp.dot(c_in, hl, preferred_element_type=F32))
                b_out = (bg_t * w_t[hd:hd + 1, :]).astype(BF16)
                upd_h = jnp.dot(b_out, xl, preferred_element_type=F32)
                dec_h = jnp.broadcast_to(e_tot[:, hd:hd + 1], (1, LANES))
                mine = lane_head == sub
                y = y_h if y is None else jnp.where(mine, y_h, y)
                upd = upd_h if upd is None else jnp.where(mine, upd_h, upd)
                dec = dec_h if dec is None else jnp.where(lane_head_row == sub, dec_h, dec)
            y_ref[0, :, g * gw + lt * LANES:g * gw + (lt + 1) * LANES] = y.astype(y_ref.dtype)
            h_ref[g, :, cols] = hg[:, cols] * dec + upd


def _ssd_scan(xbc, dt, dt_bias, a_log, reverse):
    bsz, t, _ = xbc.shape
    heads = dt.shape[2] // 2
    d_ssd = heads * SSD_HEAD_DIM
    nc, n_ctx = t // CHUNK, CTX_LEN // CHUNK
    order = lambda b, s: (b, _chunk_order(s, n_ctx, nc, reverse), 0)
    return pl.pallas_call(
        functools.partial(_ssd_kernel, reverse=reverse, d_ssd=d_ssd, heads=heads),
        grid=(bsz, nc),
        in_specs=[pl.BlockSpec((1, CHUNK, xbc.shape[2]), order),
                  pl.BlockSpec((1, CHUNK, 2 * heads), order),
                  pl.BlockSpec((1, 2 * heads), lambda b, s: (0, 0)),
                  pl.BlockSpec((1, 2 * heads), lambda b, s: (0, 0))],
        out_specs=pl.BlockSpec((1, CHUNK, d_ssd), order),
        out_shape=jax.ShapeDtypeStruct((bsz, t, d_ssd), BF16),
        scratch_shapes=[pltpu.VMEM((SSD_GROUPS, SSD_STATE, d_ssd // SSD_GROUPS), F32)],
        name="ssd_scan_bwd" if reverse else "ssd_scan_fwd",
        compiler_params=_cparams("parallel", "arbitrary"),
    )(xbc, dt, dt_bias.reshape(1, -1), a_log.reshape(1, -1))


def _ret_kernel(q_ref, k_ref, v_ref, y_ref, h_ref, *, reverse, v_dim):
    @pl.when(pl.program_id(1) == 0)
    def _():
        h_ref[...] = jnp.zeros_like(h_ref)

    mask = _tri_mask(reverse)
    i = lax.broadcasted_iota(jnp.int32, (CHUNK, CHUNK), 0)
    j = lax.broadcasted_iota(jnp.int32, (CHUNK, CHUNK), 1)
    dist = (jnp.abs(i - j)).astype(F32)
    n_in = ((CHUNK - i) if reverse else (i + 1)).astype(F32)
    n_out = (i if reverse else (CHUNK - 1 - i)).astype(F32)
    reps = v_dim // CHUNK
    heads = range(RET_HEADS)
    lgs = [math.log1p(-(2.0 ** (-5.0 - (RET_HEADS - 1 - h if reverse else h)))) for h in heads]
    qs = [q_ref[0, :, h * RET_QK_DIM:(h + 1) * RET_QK_DIM] for h in heads]
    ks = [k_ref[0, :, h * RET_QK_DIM:(h + 1) * RET_QK_DIM] for h in heads]
    vs = [v_ref[0, :, h * v_dim:(h + 1) * v_dim].astype(BF16) for h in heads]
    hs = [h_ref[h] for h in heads]
    scores = [lax.dot_general(qs[h], ks[h], (((1,), (1,)), ((), ())),
                              preferred_element_type=F32) for h in heads]
    inter = [jnp.dot(qs[h], hs[h].astype(BF16), preferred_element_type=F32) for h in heads]
    ps = [(scores[h] * jnp.where(mask, jnp.exp(lgs[h] * dist), 0.0)).astype(BF16) for h in heads]
    k_out = [(ks[h].astype(F32) * jnp.exp(lgs[h] * n_out)).astype(BF16) for h in heads]
    intra = [jnp.dot(ps[h], vs[h], preferred_element_type=F32) for h in heads]
    upd = [lax.dot_general(k_out[h], vs[h], (((0,), (0,)), ((), ())),
                           preferred_element_type=F32) for h in heads]
    for h in heads:
        e_in = jnp.concatenate([jnp.exp(lgs[h] * n_in)] * reps, axis=1)
        y_ref[0, :, h * v_dim:(h + 1) * v_dim] = (intra[h] + e_in * inter[h]).astype(y_ref.dtype)
    for h in heads:
        h_ref[h] = hs[h] * math.exp(lgs[h] * CHUNK) + upd[h]


def _ret_scan(qk, pr, v_blk, v_width, reverse):
    bsz, t, qk_width = qk.shape
    qk_width //= 2
    v_dim = v_width // RET_HEADS
    nc, n_ctx = t // CHUNK, CTX_LEN // CHUNK
    order = lambda b, s: (b, _chunk_order(s, n_ctx, nc, reverse), 0)
    col = lambda c: (lambda b, s: (b, _chunk_order(s, n_ctx, nc, reverse), c))
    return pl.pallas_call(
        functools.partial(_ret_kernel, reverse=reverse, v_dim=v_dim),
        grid=(bsz, nc),
        in_specs=[pl.BlockSpec((1, CHUNK, qk_width), col(0)),
                  pl.BlockSpec((1, CHUNK, qk_width), col(1)),
                  pl.BlockSpec((1, CHUNK, v_width), col(v_blk))],
        out_specs=pl.BlockSpec((1, CHUNK, v_width), order),
        out_shape=jax.ShapeDtypeStruct((bsz, t, v_width), BF16),
        scratch_shapes=[pltpu.VMEM((RET_HEADS, RET_QK_DIM, v_dim), F32)],
        name="ret_scan_bwd" if reverse else "ret_scan_fwd",
        compiler_params=_cparams("parallel", "arbitrary"),
    )(qk, qk, pr)


def _even_out_kernel(ysf_ref, ysb_ref, yrf_ref, yrb_ref, xs_ref, z_ref, g_ref,
                     dsk_ref, sn_ref, rn_ref, o_ref, *, d_ssd, v_dim):
    ys = (ysf_ref[0].astype(F32) + ysb_ref[0].astype(F32)
          + dsk_ref[...] * xs_ref[0].astype(F32))
    ys = ys * _silu(z_ref[0])
    ys = ys * lax.rsqrt(jnp.mean(ys * ys, axis=-1, keepdims=True) + EPS) * sn_ref[...]
    o_ref[0, :, :d_ssd] = ys.astype(o_ref.dtype)
    for h in range(RET_HEADS):
        cols = slice(h * v_dim, (h + 1) * v_dim)
        yr = yrf_ref[0, :, cols].astype(F32) + yrb_ref[0, :, cols].astype(F32)
        mu = jnp.mean(yr, axis=-1, keepdims=True)
        dev = yr - mu
        var = jnp.mean(dev * dev, axis=-1, keepdims=True)
        yn = dev * lax.rsqrt(var + EPS) * rn_ref[:, cols]
        o_ref[0, :, d_ssd + h * v_dim:d_ssd + (h + 1) * v_dim] = (
            yn * _silu(g_ref[0, :, cols])).astype(o_ref.dtype)


def _even_out(ysf, ysb, yrf, yrb, xbc, pa, pb, z_blk, g_blk, d_skip_row, ssd_norm, ret_norm):
    bsz, t, d_ssd = ysf.shape
    d_ret = yrf.shape[2]
    r = CHUNK
    blk = lambda w, c: pl.BlockSpec((1, r, w), lambda b, i: (b, i, c))
    row = lambda w: pl.BlockSpec((1, w), lambda b, i: (0, 0))
    return pl.pallas_call(
        functools.partial(_even_out_kernel, d_ssd=d_ssd, v_dim=d_ret // RET_HEADS),
        grid=(bsz, t // r),
        in_specs=[blk(d_ssd, 0), blk(d_ssd, 0), blk(d_ret, 0), blk(d_ret, 0),
                  blk(d_ssd, 0), blk(d_ssd, z_blk), blk(d_ret, g_blk),
                  row(d_ssd), row(d_ssd), row(d_ret)],
        out_specs=blk(d_ssd + d_ret, 0),
        out_shape=jax.ShapeDtypeStruct((bsz, t, d_ssd + d_ret), BF16),
        name="even_out",
        compiler_params=_cparams("parallel", "parallel"),
    )(ysf, ysb, yrf, yrb, xbc, pa, pb, d_skip_row, ssd_norm.reshape(1, -1),
      ret_norm.reshape(1, -1))


def _lru_kernel(*refs, reverse, n_tiles, final):
    if final:
        (x_ref, p_ref, n_ref, cw_ref, cb_ref, gw_ref, gb_ref, lam_ref, yf_ref, gate_ref,
         o_ref, carry_ref, pad_ref, as_ref, us_ref) = refs
    else:
        (x_ref, p_ref, n_ref, cw_ref, cb_ref, gw_ref, gb_ref, lam_ref,
         o_ref, carry_ref, pad_ref, as_ref, us_ref) = refs
    step = pl.program_id(2)
    tile = _chunk_order(step, CTX_LEN // ROW_TILE, n_tiles, reverse)

    @pl.when(step == 0)
    def _():
        carry_ref[...] = jnp.zeros_like(carry_ref)

    xf = _conv_taps(x_ref[0], p_ref[0], n_ref[0], cw_ref, cb_ref, tile, n_tiles, pad_ref)
    r_rows, cb = xf.shape
    xb = xf.astype(BF16)
    rate = (-LRU_C / math.log(2.0)) * _softplus(-lam_ref[0])
    a_parts, u_parts = [], []
    for kb in range(cb // RNN_BLOCK):
        cols = slice(kb * RNN_BLOCK, (kb + 1) * RNN_BLOCK)
        xk = xb[:, cols]
        rg = jax.nn.sigmoid(jnp.dot(xk, gw_ref[0, 0, kb], preferred_element_type=F32)
                            + gb_ref[0, 0:1, cols])
        ig = jax.nn.sigmoid(jnp.dot(xk, gw_ref[0, 1, kb], preferred_element_type=F32)
                            + gb_ref[0, 1:2, cols])
        a = jnp.exp2(rate[:, cols] * rg)
        a_parts.append(a)
        s = 1.0 - a * a
        root = s * lax.rsqrt(jnp.maximum(s, jnp.finfo(F32).tiny))
        u_parts.append(root * (ig * xf[:, cols]))
    a = jnp.concatenate(a_parts, axis=1)
    u = jnp.concatenate(u_parts, axis=1)

    body = 0 if reverse else 8
    pad = r_rows if reverse else 0
    _slab_store(as_ref, pad, jnp.ones((8, cb), F32))
    _slab_store(us_ref, pad, jnp.zeros((8, cb), F32))
    for s in (1, 2, 4):
        _slab_store(as_ref, body, a)
        _slab_store(us_ref, body, u)
        src = body + s if reverse else body - s
        u = a * _slab_load(us_ref, src, r_rows) + u
        a = a * _slab_load(as_ref, src, r_rows)
    h = carry_ref[...]
    n_grp = r_rows // 8
    outs = [None] * n_grp
    for v in (range(n_grp - 1, -1, -1) if reverse else range(n_grp)):
        h = a[v * 8:(v + 1) * 8] * h + u[v * 8:(v + 1) * 8]
        outs[v] = h
    last = h[0:1] if reverse else h[7:8]
    carry_ref[...] = jnp.broadcast_to(last, carry_ref.shape)
    y = jnp.concatenate(outs, axis=0)
    if final:
        o_ref[0] = ((yf_ref[0] + y) * _silu(gate_ref[0])).astype(o_ref.dtype)
    else:
        o_ref[0] = y


def _lru_scan(pr, x_blk0, conv_w, conv_b, gate_w, gate_b, lam, reverse, yf=None, gate_blk0=0):
    bsz, t, _ = pr.shape
    d_rnn = conv_w.shape[1]
    cb = min(COL_TILE, d_rnn)
    r = ROW_TILE
    n_tiles = t // r
    d = 1 if reverse else 0
    final = yf is not None
    tile_of = lambda s: _chunk_order(s, CTX_LEN // r, n_tiles, reverse)
    rb = r // 8
    last = n_tiles * rb - 1
    in_specs = [
        pl.BlockSpec((1, r, cb), lambda b, j, s: (b, tile_of(s), x_blk0 + j)),
        pl.BlockSpec((1, 8, cb), lambda b, j, s: (b, jnp.maximum(tile_of(s) * rb - 1, 0), x_blk0 + j)),
        pl.BlockSpec((1, 8, cb), lambda b, j, s: (b, jnp.minimum((tile_of(s) + 1) * rb, last), x_blk0 + j)),
        pl.BlockSpec((CONV_W, cb), lambda b, j, s: (0, j)),
        pl.BlockSpec((1, cb), lambda b, j, s: (0, j)),
        pl.BlockSpec((1, 2, cb // RNN_BLOCK, RNN_BLOCK, RNN_BLOCK), lambda b, j, s: (d, 0, j, 0, 0)),
        pl.BlockSpec((1, 2, cb), lambda b, j, s: (d, 0, j)),
        pl.BlockSpec((1, 1, cb), lambda b, j, s: (d, 0, j)),
    ]
    args = [pr, pr, pr, conv_w, conv_b.reshape(1, d_rnn), gate_w, gate_b, lam.reshape(2, 1, d_rnn)]
    if final:
        in_specs += [pl.BlockSpec((1, r, cb), lambda b, j, s: (b, tile_of(s), j)),
                     pl.BlockSpec((1, r, cb), lambda b, j, s: (b, tile_of(s), gate_blk0 + j))]
        args += [yf, pr]
    return pl.pallas_call(
        functools.partial(_lru_kernel, reverse=reverse, n_tiles=n_tiles, final=final),
        grid=(bsz, d_rnn // cb, n_tiles),
        in_specs=in_specs,
        out_specs=pl.BlockSpec((1, r, cb), lambda b, j, s: (b, tile_of(s), j)),
        out_shape=jax.ShapeDtypeStruct((bsz, t, d_rnn), BF16 if final else F32),
        scratch_shapes=[pltpu.VMEM((8, cb), F32), pltpu.VMEM((cb // LANES, r + 16, LANES), F32),
                        pltpu.VMEM((cb // LANES, r + 8, LANES), F32),
                        pltpu.VMEM((cb // LANES, r + 8, LANES), F32)],
        name="lru_scan_bwd" if reverse else "lru_scan_fwd",
        compiler_params=_cparams("parallel", "parallel", "arbitrary"),
    )(*args)


def _even_layer(h2, bsz, t, w_in_all, layer, conv_w, conv_b, dt_bias, a_log, d_skip, ssd_norm,
                ret_norm, rope_tabs):
    d_model = h2.shape[1]
    d_ssd = d_model
    d_ret = d_model
    gn = SSD_GROUPS * SSD_STATE
    heads = d_ssd // SSD_HEAD_DIM
    qk = RET_HEADS * RET_QK_DIM
    n_a, n_dt = 2 * d_ssd + 2 * gn, 2 * heads
    n_b = w_in_all.shape[2] - n_a - n_dt
    pa = _matmul(h2, w_in_all, 512, layer=layer, col0=0, n=n_a).reshape(bsz, t, -1)
    qk_flat, vg_flat = _matmul_rope(h2, w_in_all, 512, layer, n_a + n_dt, 2 * qk, n_b - 2 * qk,
                                    rope_tabs, RET_QK_DIM ** -0.5)
    qkr = qk_flat.reshape(bsz, t, 2 * qk)
    vg = vg_flat.reshape(bsz, t, -1)
    dt = _matmul(h2, w_in_all, n_dt, layer=layer, col0=n_a, n=n_dt).reshape(bsz, t, n_dt)
    z_blk, v_blk, g_blk = 0, 0, 1

    cw = math.gcd(math.gcd(d_ssd, 2 * gn), COL_TILE)
    xbc = _conv_silu(pa, conv_w, conv_b, lambda j: d_ssd // cw + j, d_ssd + 2 * gn, cw)

    ysf = _ssd_scan(xbc, dt, dt_bias, a_log, False)
    ysb = _ssd_scan(xbc, dt, dt_bias, a_log, True)
    yrf = _ret_scan(qkr, vg, v_blk, d_ret, False)
    yrb = _ret_scan(qkr, vg, v_blk, d_ret, True)
    d_skip_row = jnp.repeat(d_skip, SSD_HEAD_DIM).reshape(1, d_ssd)
    return _even_out(ysf, ysb, yrf, yrb, xbc, pa, vg, z_blk, g_blk, d_skip_row, ssd_norm,
                     ret_norm)


def _odd_layer(h2, bsz, t, w_in_all, layer, conv_w, conv_b, gate_w, gate_b, lam):
    d_rnn = conv_w.shape[1]
    pr = _matmul(h2, w_in_all, 512, layer=layer).reshape(bsz, t, 2 * d_rnn)
    cb = min(COL_TILE, d_rnn)
    gw = gate_w.astype(BF16)
    yf = _lru_scan(pr, d_rnn // cb, conv_w, conv_b, gw, gate_b, lam, False)
    return _lru_scan(pr, d_rnn // cb, conv_w, conv_b, gw, gate_b, lam, True, yf=yf, gate_blk0=0)


def kernel(x, c, ctx, c_ctx, ada_w, ada_b, norm_pre, norm_post, e_w_in, e_conv_w, e_conv_b,
           e_dt_bias, e_a_log, e_d_skip, e_ssd_norm, e_ret_norm, e_w_out, o_w_in, o_conv_w,
           o_conv_b, o_gate_w, o_gate_b, o_lambda, o_w_out):
    bsz, seq, d = x.shape
    depth = ada_w.shape[0]
    t = CTX_LEN + seq
    m = bsz * t
    mod = _modulation(jnp.concatenate([c, c_ctx[None]], axis=0), ada_w, ada_b)
    n_ctx_blk, n_lat_blk = CTX_LEN // ROW_TILE, seq // ROW_TILE
    mod_lat = jnp.broadcast_to(mod[:, :bsz].reshape(depth, bsz, 1, 3, d),
                               (depth, bsz, n_lat_blk, 3, d))
    mod_ctx = jnp.broadcast_to(mod[:, bsz].reshape(depth, 1, 1, 3, d),
                               (depth, bsz, n_ctx_blk, 3, d))
    modblk = jnp.concatenate([mod_ctx, mod_lat], axis=2).reshape(depth, m // ROW_TILE, 3, d)

    rope_tabs = _rope_tables(seq, bsz)
    xs, h2 = _prenorm(ctx, x, norm_pre[0], modblk[0])
    xs, h2 = xs.reshape(m, d), h2.reshape(m, d)
    for i in range(depth):
        j = i // 2
        if i % 2 == 0:
            y = _even_layer(h2, bsz, t, e_w_in, j, e_conv_w[j], e_conv_b[j], e_dt_bias[j],
                            e_a_log[j], e_d_skip[j], e_ssd_norm[j], e_ret_norm[j], rope_tabs)
            w_out = e_w_out[j]
        else:
            y = _odd_layer(h2, bsz, t, o_w_in, j, o_conv_w[j], o_conv_b[j], o_gate_w[j],
                           o_gate_b[j], o_lambda[j])
            w_out = o_w_out[j]
        out = _matmul(y.reshape(m, -1), w_out.astype(BF16), 512)
        if i + 1 < depth:
            xs, h2 = _resnorm(xs, out, modblk[i], norm_post[i], norm_pre[i + 1], modblk[i + 1])
    return _final_residual(xs.reshape(bsz, t, d), out.reshape(bsz, t, d), modblk[depth - 1],
                           norm_post[depth - 1], seq)
```

```python
import functools
import math

import numpy as np
import jax
import jax.numpy as jnp
from jax import lax
from jax.experimental import pallas as pl
from jax.experimental.pallas import tpu as pltpu

F32 = jnp.float32
BF16 = jnp.bfloat16

GRID_W = 64
CTX_LEN = 256
SSD_HEAD_DIM = 64
SSD_GROUPS = 8
SSD_STATE = 128
RET_HEADS = 16
RET_QK_DIM = 128
RNN_BLOCK = 256
LRU_C = 8.0
CONV_W = 4
CHUNK = 128
ROPE_BASE = 10000.0
EPS = 1e-6

LANES = 128
MXU_COLS = 256
ROW_TILE = 256
COL_TILE = 2048
VMEM_LIMIT = 56 * 1024 * 1024
NEG_BIG = -1e30


def _cparams(*sem):
    return pltpu.CompilerParams(dimension_semantics=sem, vmem_limit_bytes=VMEM_LIMIT)


def _silu(x):
    return x * jax.nn.sigmoid(x)


def _softplus(x):
    return jnp.maximum(x, 0.0) + jnp.log1p(jnp.exp(-jnp.abs(x)))


def _mod_kernel(ct_ref, w_ref, b_ref, o_ref, *, n_vec):
    w = w_ref[0]
    o_ref[0] = jnp.zeros(o_ref.shape[1:], F32)
    for m in range(n_vec):
        col = _silu(ct_ref[:, m:m + 1])
        o_ref[0, m:m + 1, :] = jnp.sum(w * col, axis=0, keepdims=True) + b_ref[0]


def _modulation(cvecs, ada_w, ada_b):
    depth, d, n3 = ada_w.shape
    n_vec = cvecs.shape[0]
    ct = jnp.zeros((d, 8), F32).at[:, :n_vec].set(cvecs.T)
    tn = 512
    return pl.pallas_call(
        functools.partial(_mod_kernel, n_vec=n_vec),
        grid=(depth, n3 // tn),
        in_specs=[pl.BlockSpec((d, 8), lambda i, j: (0, 0)),
                  pl.BlockSpec((1, d, tn), lambda i, j: (i, 0, j)),
                  pl.BlockSpec((1, 1, tn), lambda i, j: (i, 0, j))],
        out_specs=pl.BlockSpec((1, 8, tn), lambda i, j: (i, 0, j)),
        out_shape=jax.ShapeDtypeStruct((depth, 8, n3), F32),
        name="modulation",
        compiler_params=_cparams("parallel", "parallel"),
    )(ct, ada_w, ada_b.reshape(depth, 1, n3))


def _prenorm_kernel(ctx_ref, x_ref, g_ref, mod_ref, xs_ref, h_ref, *, ctx_tiles):
    x = jnp.where(pl.program_id(1) < ctx_tiles, ctx_ref[0], x_ref[0])
    xs_ref[0] = x
    y = x * lax.rsqrt(jnp.mean(x * x, axis=-1, keepdims=True) + EPS) * g_ref[...]
    h_ref[0] = (y * (1.0 + mod_ref[0, 1:2, :]) + mod_ref[0, 0:1, :]).astype(h_ref.dtype)


def _prenorm(ctx, x, g, modblk):
    bsz, seq, d = x.shape
    r = ROW_TILE
    ctx_tiles = ctx.shape[1] // r
    n_tiles = ctx_tiles + seq // r
    tile = pl.BlockSpec((1, r, d), lambda b, i: (b, i, 0))
    return pl.pallas_call(
        functools.partial(_prenorm_kernel, ctx_tiles=ctx_tiles),
        grid=(bsz, n_tiles),
        in_specs=[pl.BlockSpec((1, r, d), lambda b, i: (b, jnp.minimum(i, ctx_tiles - 1), 0)),
                  pl.BlockSpec((1, r, d), lambda b, i: (b, jnp.maximum(i - ctx_tiles, 0), 0)),
                  pl.BlockSpec((1, d), lambda b, i: (0, 0)),
                  pl.BlockSpec((1, 3, d), lambda b, i: (b * n_tiles + i, 0, 0))],
        out_specs=[tile, tile],
        out_shape=[jax.ShapeDtypeStruct((bsz, n_tiles * r, d), F32),
                   jax.ShapeDtypeStruct((bsz, n_tiles * r, d), BF16)],
        name="prenorm",
        compiler_params=_cparams("parallel", "parallel"),
    )(ctx, x, g.reshape(1, d), modblk)


def _residual(x, y, gate, g_post):
    return x + gate * (y * lax.rsqrt(jnp.mean(y * y, axis=-1, keepdims=True) + EPS) * g_post)


def _resnorm_kernel(x_ref, y_ref, modp_ref, gpost_ref, gpre_ref, modn_ref, xn_ref, h_ref):
    xn = _residual(x_ref[...], y_ref[...], modp_ref[0, 2:3, :], gpost_ref[...])
    xn_ref[...] = xn
    hn = xn * lax.rsqrt(jnp.mean(xn * xn, axis=-1, keepdims=True) + EPS) * gpre_ref[...]
    h_ref[...] = (hn * (1.0 + modn_ref[0, 1:2, :]) + modn_ref[0, 0:1, :]).astype(h_ref.dtype)


def _resnorm(x2, y2, modp, g_post, g_pre, modn):
    m, d = x2.shape
    r = ROW_TILE
    tile = pl.BlockSpec((r, d), lambda i: (i, 0))
    vec = pl.BlockSpec((1, d), lambda i: (0, 0))
    mod = pl.BlockSpec((1, 3, d), lambda i: (i, 0, 0))
    return pl.pallas_call(
        _resnorm_kernel,
        grid=(m // r,),
        in_specs=[tile, tile, mod, vec, vec, mod],
        out_specs=[tile, tile],
        out_shape=[jax.ShapeDtypeStruct((m, d), F32), jax.ShapeDtypeStruct((m, d), BF16)],
        name="resnorm",
        compiler_params=_cparams("parallel"),
    )(x2, y2, modp, g_post.reshape(1, d), g_pre.reshape(1, d), modn)


def _final_kernel(x_ref, y_ref, modp_ref, gpost_ref, o_ref):
    o_ref[0] = _residual(x_ref[0], y_ref[0], modp_ref[0, 2:3, :], gpost_ref[...])


def _final_residual(x3, y3, modp, g_post, seq):
    bsz, t, d = x3.shape
    r = ROW_TILE
    skip, per = (t - seq) // r, t // r
    tile = pl.BlockSpec((1, r, d), lambda b, i: (b, i + skip, 0))
    return pl.pallas_call(
        _final_kernel,
        grid=(bsz, seq // r),
        in_specs=[tile, tile,
                  pl.BlockSpec((1, 3, d), lambda b, i: (b * per + i + skip, 0, 0)),
                  pl.BlockSpec((1, d), lambda b, i: (0, 0))],
        out_specs=pl.BlockSpec((1, r, d), lambda b, i: (b, i, 0)),
        out_shape=jax.ShapeDtypeStruct((bsz, seq, d), F32),
        name="final_residual",
        compiler_params=_cparams("parallel", "parallel"),
    )(x3, y3, modp, g_post.reshape(1, d))


A_TILE_BYTES = 12 * 1024 * 1024


def _mm_tile(a_ref, b_ref, cols=slice(None)):
    b = b_ref[0, :, cols] if len(b_ref.shape) == 3 else b_ref[:, cols]
    return jnp.dot(a_ref[...], b.astype(BF16), preferred_element_type=F32)


def _mm_kernel(a_ref, b_ref, o_ref):
    o_ref[...] = _mm_tile(a_ref, b_ref).astype(o_ref.dtype)


def _rope_cols(x, cos, sin, first):
    q4 = RET_QK_DIM // 4
    partner = jnp.where(first, pltpu.roll(x, RET_QK_DIM - q4, 1), pltpu.roll(x, q4, 1))
    return x * cos + partner * sin


def _proj_epilogue_kernel(*refs, segs, n16, has_tabs, has_plain):
    a_ref, b_ref = refs[0], refs[1]
    pos = 2
    if has_tabs:
        cos_ref, sin_ref = refs[2], refs[3]
        pos = 4
    o16_ref = refs[pos]
    j = pl.program_id(1)
    tn = o16_ref.shape[1]

    for lo, hi, mode, scale in segs:
        @pl.when(jnp.logical_and(j >= lo, j < hi))
        def _(mode=mode, scale=scale):
            if mode == "rope":
                cos, sin = cos_ref[...], sin_ref[...]
                lane = lax.broadcasted_iota(jnp.int32, cos.shape, 1)
                first = (lane % (RET_QK_DIM // 2)) < RET_QK_DIM // 4
            for grp in range(tn // MXU_COLS):
                acc = _mm_tile(a_ref, b_ref, slice(grp * MXU_COLS, (grp + 1) * MXU_COLS))
                if mode == "rope":
                    acc = jnp.concatenate(
                        [_rope_cols(acc[:, h * RET_QK_DIM:(h + 1) * RET_QK_DIM], cos, sin, first)
                         for h in range(MXU_COLS // RET_QK_DIM)], axis=1)
                    if scale != 1.0:
                        acc = acc * scale
                elif mode == "silu":
                    acc = _silu(acc)
                o16_ref[:, grp * MXU_COLS:(grp + 1) * MXU_COLS] = acc.astype(o16_ref.dtype)

    if has_plain:
        @pl.when(j >= n16)
        def _():
            refs[pos + 1][...] = _mm_tile(a_ref, b_ref)


def _proj_operands(a, b, tn, layer, col0, n):
    m, k = a.shape
    tm = next(c for c in (1536, 1024, 768, 512, 256)
              if m % c == 0 and c * k * a.dtype.itemsize <= A_TILE_BYTES)
    if layer is not None and (col0 % LANES or tn % LANES):
        b, layer = b[layer, :, col0:col0 + n].astype(BF16), None
    if layer is None:
        b_spec = pl.BlockSpec((k, tn), lambda i, j: (0, j))
    else:
        b_spec = pl.BlockSpec(
            (pl.Element(1), pl.Element(k), pl.Element(tn)),
            lambda i, j: (layer, 0, pl.multiple_of(col0 + j * tn, LANES)))
    return tm, b, b_spec


def _matmul(a, b, tn, out_dtype=F32, layer=None, col0=0, n=None):
    m, k = a.shape
    n = b.shape[-1] if n is None else n
    tm, b, b_spec = _proj_operands(a, b, tn, layer, col0, n)
    return pl.pallas_call(
        _mm_kernel,
        grid=(m // tm, n // tn),
        in_specs=[pl.BlockSpec((tm, k), lambda i, j: (i, 0)), b_spec],
        out_specs=pl.BlockSpec((tm, tn), lambda i, j: (i, j)),
        out_shape=jax.ShapeDtypeStruct((m, n), out_dtype),
        name="proj_k%d" % k,
        compiler_params=_cparams("parallel", "arbitrary"),
    )(a, b)


def _matmul_epilogue(a, b, tn, layer, col0, parts16, n_plain, tabs=None):
    m, k = a.shape
    n16 = sum(p[0] for p in parts16)
    tm, b, b_spec = _proj_operands(a, b, tn, layer, col0, n16 + n_plain)
    segs, lo = [], 0
    for n_cols, mode, scale in parts16:
        segs.append((lo, lo + n_cols // tn, mode, scale))
        lo += n_cols // tn
    t16 = lo
    in_specs = [pl.BlockSpec((tm, k), lambda i, j: (i, 0)), b_spec]
    args = [a, b]
    if tabs is not None:
        in_specs += [pl.BlockSpec((tm, RET_QK_DIM), lambda i, j: (i, 0))] * 2
        args += list(tabs)
    out_specs = [pl.BlockSpec((tm, tn), lambda i, j: (i, jnp.minimum(j, t16 - 1)))]
    out_shape = [jax.ShapeDtypeStruct((m, n16), BF16)]
    if n_plain:
        out_specs.append(pl.BlockSpec((tm, tn), lambda i, j: (i, jnp.maximum(j - t16, 0))))
        out_shape.append(jax.ShapeDtypeStruct((m, n_plain), F32))
    outs = pl.pallas_call(
        functools.partial(_proj_epilogue_kernel, segs=tuple(segs), n16=t16,
                          has_tabs=tabs is not None, has_plain=bool(n_plain)),
        grid=(m // tm, (n16 + n_plain) // tn),
        in_specs=in_specs,
        out_specs=out_specs,
        out_shape=out_shape,
        name="proj_epi_k%d" % k,
        compiler_params=_cparams("parallel", "arbitrary"),
    )(*args)
    return outs[0], (outs[1] if n_plain else None)


def _mm_acc_kernel(a_ref, b_ref, p_ref, o_ref):
    o_ref[...] = p_ref[...] + _mm_tile(a_ref, b_ref)


def _matmul_ksplit(a, b_all, layer, tn):
    m, k = a.shape
    n = b_all.shape[2]
    kh = k // 2
    tm = next(c for c in (1536, 1024, 768, 512, 256)
              if m % c == 0 and c * kh * a.dtype.itemsize <= A_TILE_BYTES)
    out = None
    for half in range(2):
        a_spec = pl.BlockSpec((tm, kh), lambda i, j, half=half: (i, half))
        b_spec = pl.BlockSpec(
            (pl.Element(1), pl.Element(kh), pl.Element(tn)),
            lambda i, j, half=half: (layer, half * kh, pl.multiple_of(j * tn, LANES)))
        o_spec = pl.BlockSpec((tm, tn), lambda i, j: (i, j))
        first = out is None
        out = pl.pallas_call(
            _mm_kernel if first else _mm_acc_kernel,
            grid=(m // tm, n // tn),
            in_specs=[a_spec, b_spec] + ([] if first else [o_spec]),
            out_specs=o_spec,
            out_shape=jax.ShapeDtypeStruct((m, n), F32),
            name="proj_half_k%d" % k,
            compiler_params=_cparams("parallel", "arbitrary"),
        )(*([a, b_all] if first else [a, b_all, out]))
    return out


def _slab_store(ref, row0, val):
    for k in range(ref.shape[0]):
        ref[k, row0:row0 + val.shape[0], :] = val[:, k * LANES:(k + 1) * LANES]


def _slab_load(ref, row0, rows):
    return jnp.concatenate([ref[k, pl.ds(row0, rows, stride=1), :] for k in range(ref.shape[0])],
                           axis=1)


def _conv_taps(x, prev8, next8, w_ref, b_ref, tile, n_tiles, pad_ref):
    r = x.shape[0]
    ctx_tiles = CTX_LEN // r
    prev_ok = jnp.logical_and(tile != 0, tile != ctx_tiles)
    next_ok = jnp.logical_and(tile != ctx_tiles - 1, tile != n_tiles - 1)
    _slab_store(pad_ref, 0, jnp.where(prev_ok, prev8, 0.0))
    _slab_store(pad_ref, 8, x)
    _slab_store(pad_ref, 8 + r, jnp.where(next_ok, next8, 0.0))
    return (b_ref[...] + w_ref[0:1, :] * _slab_load(pad_ref, 6, r)
            + w_ref[1:2, :] * _slab_load(pad_ref, 7, r)
            + w_ref[2:3, :] * x + w_ref[3:4, :] * _slab_load(pad_ref, 9, r))


def _halo_specs(r, cw, col_map, n_tiles):
    rb = r // 8
    last = n_tiles * rb - 1
    return [
        pl.BlockSpec((1, r, cw), lambda b, t, j: (b, t, col_map(j))),
        pl.BlockSpec((1, 8, cw), lambda b, t, j: (b, jnp.maximum(t * rb - 1, 0), col_map(j))),
        pl.BlockSpec((1, 8, cw), lambda b, t, j: (b, jnp.minimum((t + 1) * rb, last), col_map(j))),
    ]


def _conv_silu_kernel(x_ref, p_ref, n_ref, w_ref, b_ref, o_ref, pad_ref, *, n_tiles):
    y = _conv_taps(x_ref[0], p_ref[0], n_ref[0], w_ref, b_ref, pl.program_id(1), n_tiles, pad_ref)
    o_ref[0] = _silu(y).astype(o_ref.dtype)


def _conv_silu(pr, conv_w, conv_b, col_map, n_cols, cw):
    bsz, t, _ = pr.shape
    r = ROW_TILE
    n_tiles = t // r
    return pl.pallas_call(
        functools.partial(_conv_silu_kernel, n_tiles=n_tiles),
        grid=(bsz, n_tiles, n_cols // cw),
        in_specs=_halo_specs(r, cw, col_map, n_tiles) + [
            pl.BlockSpec((CONV_W, cw), lambda b, t, j: (0, j)),
            pl.BlockSpec((1, cw), lambda b, t, j: (0, j))],
        out_specs=pl.BlockSpec((1, r, cw), lambda b, t, j: (b, t, j)),
        out_shape=jax.ShapeDtypeStruct((bsz, t, n_cols), BF16),
        scratch_shapes=[pltpu.VMEM((cw // LANES, r + 16, LANES), F32)],
        name="conv_silu",
        compiler_params=_cparams("parallel", "parallel", "parallel"),
    )(pr, pr, pr, conv_w, conv_b.reshape(1, n_cols))


def _rope_tables(seq, bsz):
    rows = seq // GRID_W
    row = jnp.repeat(jnp.arange(rows), GRID_W).astype(F32)
    col = jnp.tile(jnp.arange(GRID_W), rows).astype(F32)
    q4 = RET_QK_DIM // 4
    freq = ROPE_BASE ** (-jnp.arange(q4, dtype=F32) / q4)
    ang = jnp.concatenate([row[:, None] * freq, row[:, None] * freq,
                           col[:, None] * freq, col[:, None] * freq], axis=1)
    first = (jnp.arange(RET_QK_DIM) % (2 * q4)) < q4
    cos, sin = jnp.cos(ang), jnp.sin(ang)
    sin = jnp.where(first, -sin, sin)
    pad = lambda tab, v: jnp.tile(
        jnp.concatenate([jnp.full((CTX_LEN, RET_QK_DIM), v, F32), tab], axis=0), (bsz, 1))
    return pad(cos, 1.0), pad(sin, 0.0)


def _chunk_order(step, n_ctx, n_all, reverse):
    if not reverse:
        return step
    return jnp.where(step < n_ctx, n_ctx - 1 - step, n_all - 1 + n_ctx - step)


def _tri_mask(reverse):
    i = lax.broadcasted_iota(jnp.int32, (CHUNK, CHUNK), 0)
    j = lax.broadcasted_iota(jnp.int32, (CHUNK, CHUNK), 1)
    return (j >= i) if reverse else (j <= i)


def _ssd_kernel(xbc_ref, dt_ref, bias_ref, alog_ref, y_ref, h_ref, *, reverse, d_ssd, heads):
    hpg = heads // SSD_GROUPS
    gw = hpg * SSD_HEAD_DIM
    d = 1 if reverse else 0

    @pl.when(pl.program_id(1) == 0)
    def _():
        h_ref[...] = jnp.zeros_like(h_ref)

    mask = _tri_mask(reverse)
    hpt = LANES // SSD_HEAD_DIM
    lane_head = lax.broadcasted_iota(jnp.int32, (CHUNK, LANES), 1) // SSD_HEAD_DIM
    lane_head_row = lax.broadcasted_iota(jnp.int32, (1, LANES), 1) // SSD_HEAD_DIM
    lo = d * heads
    dtv = _softplus(dt_ref[0] + bias_ref[...])[:, lo:lo + heads]
    la = dtv * (-jnp.exp(alog_ref[...]))[:, lo:lo + heads]
    acs = jnp.dot(mask.astype(F32), la, precision=lax.Precision.HIGHEST,
                  preferred_element_type=F32)
    tot = acs[0:1, :] if reverse else acs[CHUNK - 1:CHUNK, :]
    e_tot = jnp.exp(tot)
    w = jnp.exp(tot - acs) * dtv
    acs_w_t = jnp.concatenate([acs, w], axis=1).T
    acs_t, w_t = acs_w_t[:heads], acs_w_t[heads:]
    dt_t = jnp.concatenate([dtv, dtv], axis=1).T[:heads]

    bm_off = d_ssd
    cm_off = d_ssd + SSD_GROUPS * SSD_STATE
    for g in range(SSD_GROUPS):
        cg = xbc_ref[0, :, cm_off + g * SSD_STATE:cm_off + (g + 1) * SSD_STATE]
        bg = xbc_ref[0, :, bm_off + g * SSD_STATE:bm_off + (g + 1) * SSD_STATE]
        scores = lax.dot_general(cg, bg, (((1,), (1,)), ((), ())), preferred_element_type=F32)
        cg32 = cg.astype(F32)
        bg_t = bg.astype(F32).T
        hg = h_ref[g]
        hg16 = hg.astype(BF16)
        xg = xbc_ref[0, :, g * gw:(g + 1) * gw]
        for lt in range(gw // LANES):
            cols = slice(lt * LANES, (lt + 1) * LANES)
            xl, hl = xg[:, cols], hg16[:, cols]
            y = upd = dec = None
            for sub in range(hpt):
                hd = g * hpg + lt * hpt + sub
                col_b = jnp.broadcast_to(acs[:, hd:hd + 1], (CHUNK, CHUNK))
                dm = jnp.exp(jnp.where(mask, col_b - acs_t[hd:hd + 1, :], NEG_BIG))
                p = (scores * dm * dt_t[hd:hd + 1, :]).astype(BF16)
                c_in = (cg32 * jnp.exp(col_b)).astype(BF16)
                y_h = (jnp.dot(p, xl, preferred_element_type=F32)
                       + jnp.dot(c_in, hl, preferred_element_type=F32))
                b_out = (bg_t * w_t[hd:hd + 1, :]).astype(BF16)
                upd_h = jnp.dot(b_out, xl, preferred_element_type=F32)
                dec_h = jnp.broadcast_to(e_tot[:, hd:hd + 1], (1, LANES))
                mine = lane_head == sub
                y = y_h if y is None else jnp.where(mine, y_h, y)
                upd = upd_h if upd is None else jnp.where(mine, upd_h, upd)
                dec = dec_h if dec is None else jnp.where(lane_head_row == sub, dec_h, dec)
            y_ref[0, :, g * gw + lt * LANES:g * gw + (lt + 1) * LANES] = y.astype(y_ref.dtype)
            h_ref[g, :, cols] = hg[:, cols] * dec + upd


def _ssd_scan(xbc, dt, dt_bias, a_log, reverse):
    bsz, t, _ = xbc.shape
    heads = dt.shape[2] // 2
    d_ssd = heads * SSD_HEAD_DIM
    nc, n_ctx = t // CHUNK, CTX_LEN // CHUNK
    order = lambda b, s: (b, _chunk_order(s, n_ctx, nc, reverse), 0)
    return pl.pallas_call(
        functools.partial(_ssd_kernel, reverse=reverse, d_ssd=d_ssd, heads=heads),
        grid=(bsz, nc),
        in_specs=[pl.BlockSpec((1, CHUNK, xbc.shape[2]), order),
                  pl.BlockSpec((1, CHUNK, 2 * heads), order),
                  pl.BlockSpec((1, 2 * heads), lambda b, s: (0, 0)),
                  pl.BlockSpec((1, 2 * heads), lambda b, s: (0, 0))],
        out_specs=pl.BlockSpec((1, CHUNK, d_ssd), order),
        out_shape=jax.ShapeDtypeStruct((bsz, t, d_ssd), BF16),
        scratch_shapes=[pltpu.VMEM((SSD_GROUPS, SSD_STATE, d_ssd // SSD_GROUPS), F32)],
        name="ssd_scan_bwd" if reverse else "ssd_scan_fwd",
        compiler_params=_cparams("parallel", "arbitrary"),
    )(xbc, dt, dt_bias.reshape(1, -1), a_log.reshape(1, -1))


def _ret_kernel(q_ref, k_ref, v_ref, y_ref, h_ref, *, reverse, v_dim):
    @pl.when(pl.program_id(1) == 0)
    def _():
        h_ref[...] = jnp.zeros_like(h_ref)

    mask = _tri_mask(reverse)
    i = lax.broadcasted_iota(jnp.int32, (CHUNK, CHUNK), 0)
    j = lax.broadcasted_iota(jnp.int32, (CHUNK, CHUNK), 1)
    dist = (jnp.abs(i - j)).astype(F32)
    n_in = ((CHUNK - i) if reverse else (i + 1)).astype(F32)
    n_out = (i if reverse else (CHUNK - 1 - i)).astype(F32)
    reps = v_dim // CHUNK
    heads = range(RET_HEADS)
    lgs = [math.log1p(-(2.0 ** (-5.0 - (RET_HEADS - 1 - h if reverse else h)))) for h in heads]
    qs = [q_ref[0, :, h * RET_QK_DIM:(h + 1) * RET_QK_DIM] for h in heads]
    ks = [k_ref[0, :, h * RET_QK_DIM:(h + 1) * RET_QK_DIM] for h in heads]
    vs = [v_ref[0, :, h * v_dim:(h + 1) * v_dim].astype(BF16) for h in heads]
    hs = [h_ref[h] for h in heads]
    scores = [lax.dot_general(qs[h], ks[h], (((1,), (1,)), ((), ())),
                              preferred_element_type=F32) for h in heads]
    inter = [jnp.dot(qs[h], hs[h].astype(BF16), preferred_element_type=F32) for h in heads]
    ps = [(scores[h] * jnp.where(mask, jnp.exp(lgs[h] * dist), 0.0)).astype(BF16) for h in heads]
    k_out = [(ks[h].astype(F32) * jnp.exp(lgs[h] * n_out)).astype(BF16) for h in heads]
    intra = [jnp.dot(ps[h], vs[h], preferred_element_type=F32) for h in heads]
    upd = [lax.dot_general(k_out[h], vs[h], (((0,), (0,)), ((), ())),
                           preferred_element_type=F32) for h in heads]
    for h in heads:
        e_in = jnp.concatenate([jnp.exp(lgs[h] * n_in)] * reps, axis=1)
        y_ref[0, :, h * v_dim:(h + 1) * v_dim] = (intra[h] + e_in * inter[h]).astype(y_ref.dtype)
    for h in heads:
        h_ref[h] = hs[h] * math.exp(lgs[h] * CHUNK) + upd[h]


def _ret_scan(qkv, v_blk, v_width, reverse):
    qk = pr = qkv
    bsz, t, _ = qkv.shape
    qk_width = RET_HEADS * RET_QK_DIM
    v_dim = v_width // RET_HEADS
    nc, n_ctx = t // CHUNK, CTX_LEN // CHUNK
    order = lambda b, s: (b, _chunk_order(s, n_ctx, nc, reverse), 0)
    col = lambda c: (lambda b, s: (b, _chunk_order(s, n_ctx, nc, reverse), c))
    return pl.pallas_call(
        functools.partial(_ret_kernel, reverse=reverse, v_dim=v_dim),
        grid=(bsz, nc),
        in_specs=[pl.BlockSpec((1, CHUNK, qk_width), col(0)),
                  pl.BlockSpec((1, CHUNK, qk_width), col(1)),
                  pl.BlockSpec((1, CHUNK, v_width), col(v_blk))],
        out_specs=pl.BlockSpec((1, CHUNK, v_width), order),
        out_shape=jax.ShapeDtypeStruct((bsz, t, v_width), BF16),
        scratch_shapes=[pltpu.VMEM((RET_HEADS, RET_QK_DIM, v_dim), F32)],
        name="ret_scan_bwd" if reverse else "ret_scan_fwd",
        compiler_params=_cparams("parallel", "arbitrary"),
    )(qk, qk, pr)


def _even_out_kernel(ysf_ref, ysb_ref, yrf_ref, yrb_ref, xs_ref, z_ref, g_ref,
                     dsk_ref, sn_ref, rn_ref, o_ref, *, d_ssd, v_dim):
    ys = (ysf_ref[0].astype(F32) + ysb_ref[0].astype(F32)
          + dsk_ref[...] * xs_ref[0].astype(F32))
    ys = ys * z_ref[0]
    ys = ys * lax.rsqrt(jnp.mean(ys * ys, axis=-1, keepdims=True) + EPS) * sn_ref[...]
    o_ref[0, :, :d_ssd] = ys.astype(o_ref.dtype)
    for h in range(RET_HEADS):
        cols = slice(h * v_dim, (h + 1) * v_dim)
        yr = yrf_ref[0, :, cols].astype(F32) + yrb_ref[0, :, cols].astype(F32)
        mu = jnp.mean(yr, axis=-1, keepdims=True)
        dev = yr - mu
        var = jnp.mean(dev * dev, axis=-1, keepdims=True)
        yn = dev * lax.rsqrt(var + EPS) * rn_ref[:, cols]
        o_ref[0, :, d_ssd + h * v_dim:d_ssd + (h + 1) * v_dim] = (
            yn * g_ref[0, :, cols]).astype(o_ref.dtype)


def _even_out(ysf, ysb, yrf, yrb, xbc, pa, pb, z_blk, g_blk, d_skip_row, ssd_norm, ret_norm):
    bsz, t, d_ssd = ysf.shape
    d_ret = yrf.shape[2]
    r = CHUNK
    blk = lambda w, c: pl.BlockSpec((1, r, w), lambda b, i: (b, i, c))
    row = lambda w: pl.BlockSpec((1, w), lambda b, i: (0, 0))
    return pl.pallas_call(
        functools.partial(_even_out_kernel, d_ssd=d_ssd, v_dim=d_ret // RET_HEADS),
        grid=(bsz, t // r),
        in_specs=[blk(d_ssd, 0), blk(d_ssd, 0), blk(d_ret, 0), blk(d_ret, 0),
                  blk(d_ssd, 0), blk(d_ssd, z_blk), blk(d_ret, g_blk),
                  row(d_ssd), row(d_ssd), row(d_ret)],
        out_specs=blk(d_ssd + d_ret, 0),
        out_shape=jax.ShapeDtypeStruct((bsz, t, d_ssd + d_ret), BF16),
        name="even_out",
        compiler_params=_cparams("parallel", "parallel"),
    )(ysf, ysb, yrf, yrb, xbc, pa, pb, d_skip_row, ssd_norm.reshape(1, -1),
      ret_norm.reshape(1, -1))


def _lru_kernel(*refs, reverse, n_tiles, final):
    if final:
        (x_ref, p_ref, n_ref, cw_ref, cb_ref, gw_ref, gb_ref, lam_ref, yf_ref, gate_ref,
         o_ref, carry_ref, pad_ref, as_ref, us_ref) = refs
    else:
        (x_ref, p_ref, n_ref, cw_ref, cb_ref, gw_ref, gb_ref, lam_ref,
         o_ref, carry_ref, pad_ref, as_ref, us_ref) = refs
    step = pl.program_id(2)
    tile = _chunk_order(step, CTX_LEN // ROW_TILE, n_tiles, reverse)

    @pl.when(step == 0)
    def _():
        carry_ref[...] = jnp.zeros_like(carry_ref)

    xf = _conv_taps(x_ref[0], p_ref[0], n_ref[0], cw_ref, cb_ref, tile, n_tiles, pad_ref)
    r_rows, cb = xf.shape
    xb = xf.astype(BF16)
    rate = (-LRU_C / math.log(2.0)) * _softplus(-lam_ref[0])
    a_parts, u_parts = [], []
    for kb in range(cb // RNN_BLOCK):
        cols = slice(kb * RNN_BLOCK, (kb + 1) * RNN_BLOCK)
        xk = xb[:, cols]
        rg = jax.nn.sigmoid(jnp.dot(xk, gw_ref[0, 0, kb], preferred_element_type=F32)
                            + gb_ref[0, 0:1, cols])
        ig = jax.nn.sigmoid(jnp.dot(xk, gw_ref[0, 1, kb], preferred_element_type=F32)
                            + gb_ref[0, 1:2, cols])
        a = jnp.exp2(rate[:, cols] * rg)
        a_parts.append(a)
        s = 1.0 - a * a
        root = s * lax.rsqrt(jnp.maximum(s, jnp.finfo(F32).tiny))
        u_parts.append(root * (ig * xf[:, cols]))
    a = jnp.concatenate(a_parts, axis=1)
    u = jnp.concatenate(u_parts, axis=1)

    body = 0 if reverse else 8
    pad = r_rows if reverse else 0
    _slab_store(as_ref, pad, jnp.ones((8, cb), F32))
    _slab_store(us_ref, pad, jnp.zeros((8, cb), F32))
    for s in (1, 2, 4):
        _slab_store(as_ref, body, a)
        _slab_store(us_ref, body, u)
        src = body + s if reverse else body - s
        u = a * _slab_load(us_ref, src, r_rows) + u
        a = a * _slab_load(as_ref, src, r_rows)
    h = carry_ref[...]
    n_grp = r_rows // 8
    outs = [None] * n_grp
    for v in (range(n_grp - 1, -1, -1) if reverse else range(n_grp)):
        h = a[v * 8:(v + 1) * 8] * h + u[v * 8:(v + 1) * 8]
        outs[v] = h
    last = h[0:1] if reverse else h[7:8]
    carry_ref[...] = jnp.broadcast_to(last, carry_ref.shape)
    y = jnp.concatenate(outs, axis=0)
    if final:
        o_ref[0] = ((yf_ref[0] + y) * gate_ref[0]).astype(o_ref.dtype)
    else:
        o_ref[0] = y


def _lru_scan(pr, conv_w, conv_b, gate_w, gate_b, lam, reverse, yf=None, gate=None):
    bsz, t, _ = pr.shape
    x_blk0 = gate_blk0 = 0
    d_rnn = conv_w.shape[1]
    cb = min(COL_TILE, d_rnn)
    r = ROW_TILE
    n_tiles = t // r
    d = 1 if reverse else 0
    final = yf is not None
    tile_of = lambda s: _chunk_order(s, CTX_LEN // r, n_tiles, reverse)
    rb = r // 8
    last = n_tiles * rb - 1
    in_specs = [
        pl.BlockSpec((1, r, cb), lambda b, j, s: (b, tile_of(s), x_blk0 + j)),
        pl.BlockSpec((1, 8, cb), lambda b, j, s: (b, jnp.maximum(tile_of(s) * rb - 1, 0), x_blk0 + j)),
        pl.BlockSpec((1, 8, cb), lambda b, j, s: (b, jnp.minimum((tile_of(s) + 1) * rb, last), x_blk0 + j)),
        pl.BlockSpec((CONV_W, cb), lambda b, j, s: (0, j)),
        pl.BlockSpec((1, cb), lambda b, j, s: (0, j)),
        pl.BlockSpec((1, 2, cb // RNN_BLOCK, RNN_BLOCK, RNN_BLOCK), lambda b, j, s: (d, 0, j, 0, 0)),
        pl.BlockSpec((1, 2, cb), lambda b, j, s: (d, 0, j)),
        pl.BlockSpec((1, 1, cb), lambda b, j, s: (d, 0, j)),
    ]
    args = [pr, pr, pr, conv_w, conv_b.reshape(1, d_rnn), gate_w, gate_b, lam.reshape(2, 1, d_rnn)]
    if final:
        in_specs += [pl.BlockSpec((1, r, cb), lambda b, j, s: (b, tile_of(s), j)),
                     pl.BlockSpec((1, r, cb), lambda b, j, s: (b, tile_of(s), gate_blk0 + j))]
        args += [yf, gate]
    return pl.pallas_call(
        functools.partial(_lru_kernel, reverse=reverse, n_tiles=n_tiles, final=final),
        grid=(bsz, d_rnn // cb, n_tiles),
        in_specs=in_specs,
        out_specs=pl.BlockSpec((1, r, cb), lambda b, j, s: (b, tile_of(s), j)),
        out_shape=jax.ShapeDtypeStruct((bsz, t, d_rnn), BF16 if final else F32),
        scratch_shapes=[pltpu.VMEM((8, cb), F32), pltpu.VMEM((cb // LANES, r + 16, LANES), F32),
                        pltpu.VMEM((cb // LANES, r + 8, LANES), F32),
                        pltpu.VMEM((cb // LANES, r + 8, LANES), F32)],
        name="lru_scan_bwd" if reverse else "lru_scan_fwd",
        compiler_params=_cparams("parallel", "parallel", "arbitrary"),
    )(*args)


def _even_layer(h2, bsz, t, w_in_all, layer, conv_w, conv_b, dt_bias, a_log, d_skip, ssd_norm,
                ret_norm, rope_tabs):
    d_model = h2.shape[1]
    d_ssd = d_model
    d_ret = d_model
    gn = SSD_GROUPS * SSD_STATE
    heads = d_ssd // SSD_HEAD_DIM
    qk = RET_HEADS * RET_QK_DIM
    n_a, n_dt = 2 * d_ssd + 2 * gn, 2 * heads
    n_b = w_in_all.shape[2] - n_a - n_dt
    assert n_b == 2 * qk + 2 * d_ret
    zs, xbc_pre = _matmul_epilogue(h2, w_in_all, 512, layer, 0, [(d_ssd, "silu", 1.0)],
                                   d_ssd + 2 * gn)
    qkvg, _ = _matmul_epilogue(
        h2, w_in_all, 512, layer, n_a + n_dt,
        [(qk, "rope", 1.0), (qk, "rope", RET_QK_DIM ** -0.5), (d_ret, "cast", 1.0),
         (d_ret, "silu", 1.0)], 0, tabs=rope_tabs)
    zs = zs.reshape(bsz, t, d_ssd)
    xbc_pre = xbc_pre.reshape(bsz, t, -1)
    qkvg = qkvg.reshape(bsz, t, -1)
    dt = _matmul(h2, w_in_all, n_dt, layer=layer, col0=n_a, n=n_dt).reshape(bsz, t, n_dt)
    v_blk = 2 * qk // d_ret
    g_blk = v_blk + 1

    cw = math.gcd(math.gcd(d_ssd, 2 * gn), COL_TILE)
    xbc = _conv_silu(xbc_pre, conv_w, conv_b, lambda j: j, d_ssd + 2 * gn, cw)

    ysf = _ssd_scan(xbc, dt, dt_bias, a_log, False)
    ysb = _ssd_scan(xbc, dt, dt_bias, a_log, True)
    yrf = _ret_scan(qkvg, v_blk, d_ret, False)
    yrb = _ret_scan(qkvg, v_blk, d_ret, True)
    d_skip_row = jnp.repeat(d_skip, SSD_HEAD_DIM).reshape(1, d_ssd)
    return _even_out(ysf, ysb, yrf, yrb, xbc, zs, qkvg, 0, g_blk, d_skip_row, ssd_norm,
                     ret_norm)


def _odd_layer(h2, bsz, t, w_in_all, layer, conv_w, conv_b, gate_w, gate_b, lam):
    d_rnn = conv_w.shape[1]
    gate, xr = _matmul_epilogue(h2, w_in_all, 512, layer, 0, [(d_rnn, "silu", 1.0)], d_rnn)
    gate = gate.reshape(bsz, t, d_rnn)
    xr = xr.reshape(bsz, t, d_rnn)
    gw = gate_w.astype(BF16)
    yf = _lru_scan(xr, conv_w, conv_b, gw, gate_b, lam, False)
    return _lru_scan(xr, conv_w, conv_b, gw, gate_b, lam, True, yf=yf, gate=gate)


def kernel(x, c, ctx, c_ctx, ada_w, ada_b, norm_pre, norm_post, e_w_in, e_conv_w, e_conv_b,
           e_dt_bias, e_a_log, e_d_skip, e_ssd_norm, e_ret_norm, e_w_out, o_w_in, o_conv_w,
           o_conv_b, o_gate_w, o_gate_b, o_lambda, o_w_out):
    bsz, seq, d = x.shape
    depth = ada_w.shape[0]
    t = CTX_LEN + seq
    m = bsz * t
    mod = _modulation(jnp.concatenate([c, c_ctx[None]], axis=0), ada_w, ada_b)
    n_ctx_blk, n_lat_blk = CTX_LEN // ROW_TILE, seq // ROW_TILE
    mod_lat = jnp.broadcast_to(mod[:, :bsz].reshape(depth, bsz, 1, 3, d),
                               (depth, bsz, n_lat_blk, 3, d))
    mod_ctx = jnp.broadcast_to(mod[:, bsz].reshape(depth, 1, 1, 3, d),
                               (depth, bsz, n_ctx_blk, 3, d))
    modblk = jnp.concatenate([mod_ctx, mod_lat], axis=2).reshape(depth, m // ROW_TILE, 3, d)

    rope_tabs = _rope_tables(seq, bsz)
    xs, h2 = _prenorm(ctx, x, norm_pre[0], modblk[0])
    xs, h2 = xs.reshape(m, d), h2.reshape(m, d)
    for i in range(depth):
        j = i // 2
        if i % 2 == 0:
            y = _even_layer(h2, bsz, t, e_w_in, j, e_conv_w[j], e_conv_b[j], e_dt_bias[j],
                            e_a_log[j], e_d_skip[j], e_ssd_norm[j], e_ret_norm[j], rope_tabs)
            w_out = e_w_out
        else:
            y = _odd_layer(h2, bsz, t, o_w_in, j, o_conv_w[j], o_conv_b[j], o_gate_w[j],
                           o_gate_b[j], o_lambda[j])
            w_out = o_w_out
        out = _matmul_ksplit(y.reshape(m, -1), w_out, j, 512)
        if i + 1 < depth:
            xs, h2 = _resnorm(xs, out, modblk[i], norm_post[i], norm_pre[i + 1], modblk[i + 1])
    return _final_residual(xs.reshape(bsz, t, d), out.reshape(bsz, t, d), modblk[depth - 1],
                           norm_post[depth - 1], seq)
```

```python
import functools
import math

import numpy as np
import jax
import jax.numpy as jnp
from jax import lax
from jax.experimental import pallas as pl
from jax.experimental.pallas import tpu as pltpu

F32 = jnp.float32
BF16 = jnp.bfloat16

GRID_W = 64
CTX_LEN = 256
SSD_HEAD_DIM = 64
SSD_GROUPS = 8
SSD_STATE = 128
RET_HEADS = 16
RET_QK_DIM = 128
RNN_BLOCK = 256
LRU_C = 8.0
CONV_W = 4
CHUNK = 128
ROPE_BASE = 10000.0
EPS = 1e-6

LANES = 128
MXU_COLS = 256
ROW_TILE = 256
COL_TILE = 2048
SCAN_ROWS = 2 * CHUNK
VMEM_LIMIT = 56 * 1024 * 1024
NEG_BIG = -1e30


def _cparams(*sem):
    return pltpu.CompilerParams(dimension_semantics=sem, vmem_limit_bytes=VMEM_LIMIT)


def _silu(x):
    return x * jax.nn.sigmoid(x)


def _softplus(x):
    return jnp.maximum(x, 0.0) + jnp.log1p(jnp.exp(-jnp.abs(x)))


def _mod_kernel(ct_ref, w_ref, b_ref, o_ref, *, n_vec):
    w = w_ref[0]
    o_ref[0] = jnp.zeros(o_ref.shape[1:], F32)
    for m in range(n_vec):
        col = _silu(ct_ref[:, m:m + 1])
        o_ref[0, m:m + 1, :] = jnp.sum(w * col, axis=0, keepdims=True) + b_ref[0]


def _modulation(cvecs, ada_w, ada_b):
    depth, d, n3 = ada_w.shape
    n_vec = cvecs.shape[0]
    ct = jnp.zeros((d, 8), F32).at[:, :n_vec].set(cvecs.T)
    tn = 1024
    return pl.pallas_call(
        functools.partial(_mod_kernel, n_vec=n_vec),
        grid=(depth, n3 // tn),
        in_specs=[pl.BlockSpec((d, 8), lambda i, j: (0, 0)),
                  pl.BlockSpec((1, d, tn), lambda i, j: (i, 0, j)),
                  pl.BlockSpec((1, 1, tn), lambda i, j: (i, 0, j))],
        out_specs=pl.BlockSpec((1, 8, tn), lambda i, j: (i, 0, j)),
        out_shape=jax.ShapeDtypeStruct((depth, 8, n3), F32),
        name="modulation",
        compiler_params=_cparams("parallel", "parallel"),
    )(ct, ada_w, ada_b.reshape(depth, 1, n3))


def _prenorm_kernel(ctx_ref, x_ref, g_ref, mod_ref, xs_ref, h_ref, *, ctx_tiles):
    x = jnp.where(pl.program_id(1) < ctx_tiles, ctx_ref[0], x_ref[0])
    xs_ref[0] = x
    y = x * lax.rsqrt(jnp.mean(x * x, axis=-1, keepdims=True) + EPS) * g_ref[...]
    h_ref[0] = (y * (1.0 + mod_ref[0, 1:2, :]) + mod_ref[0, 0:1, :]).astype(h_ref.dtype)


def _prenorm(ctx, x, g, modblk):
    bsz, seq, d = x.shape
    r = ROW_TILE
    ctx_tiles = ctx.shape[1] // r
    n_tiles = ctx_tiles + seq // r
    tile = pl.BlockSpec((1, r, d), lambda b, i: (b, i, 0))
    return pl.pallas_call(
        functools.partial(_prenorm_kernel, ctx_tiles=ctx_tiles),
        grid=(bsz, n_tiles),
        in_specs=[pl.BlockSpec((1, r, d), lambda b, i: (b, jnp.minimum(i, ctx_tiles - 1), 0)),
                  pl.BlockSpec((1, r, d), lambda b, i: (b, jnp.maximum(i - ctx_tiles, 0), 0)),
                  pl.BlockSpec((1, d), lambda b, i: (0, 0)),
                  pl.BlockSpec((1, 3, d), lambda b, i: (b * n_tiles + i, 0, 0))],
        out_specs=[tile, tile],
        out_shape=[jax.ShapeDtypeStruct((bsz, n_tiles * r, d), F32),
                   jax.ShapeDtypeStruct((bsz, n_tiles * r, d), BF16)],
        name="prenorm",
        compiler_params=_cparams("parallel", "parallel"),
    )(ctx, x, g.reshape(1, d), modblk)


def _residual(x, y, gate, g_post):
    return x + gate * (y * lax.rsqrt(jnp.mean(y * y, axis=-1, keepdims=True) + EPS) * g_post)


def _resnorm_kernel(x_ref, y_ref, modp_ref, gpost_ref, gpre_ref, modn_ref, xn_ref, h_ref):
    xn = _residual(x_ref[...], y_ref[...], modp_ref[0, 2:3, :], gpost_ref[...])
    xn_ref[...] = xn
    hn = xn * lax.rsqrt(jnp.mean(xn * xn, axis=-1, keepdims=True) + EPS) * gpre_ref[...]
    h_ref[...] = (hn * (1.0 + modn_ref[0, 1:2, :]) + modn_ref[0, 0:1, :]).astype(h_ref.dtype)


def _resnorm(x2, y2, modp, g_post, g_pre, modn):
    m, d = x2.shape
    r = ROW_TILE
    tile = pl.BlockSpec((r, d), lambda i: (i, 0))
    vec = pl.BlockSpec((1, d), lambda i: (0, 0))
    mod = pl.BlockSpec((1, 3, d), lambda i: (i, 0, 0))
    return pl.pallas_call(
        _resnorm_kernel,
        grid=(m // r,),
        in_specs=[tile, tile, mod, vec, vec, mod],
        out_specs=[tile, tile],
        out_shape=[jax.ShapeDtypeStruct((m, d), F32), jax.ShapeDtypeStruct((m, d), BF16)],
        name="resnorm",
        compiler_params=_cparams("parallel"),
    )(x2, y2, modp, g_post.reshape(1, d), g_pre.reshape(1, d), modn)


def _final_kernel(x_ref, y_ref, modp_ref, gpost_ref, o_ref):
    o_ref[0] = _residual(x_ref[0], y_ref[0], modp_ref[0, 2:3, :], gpost_ref[...])


def _final_residual(x3, y3, modp, g_post, seq):
    bsz, t, d = x3.shape
    r = ROW_TILE
    skip, per = (t - seq) // r, t // r
    tile = pl.BlockSpec((1, r, d), lambda b, i: (b, i + skip, 0))
    return pl.pallas_call(
        _final_kernel,
        grid=(bsz, seq // r),
        in_specs=[tile, tile,
                  pl.BlockSpec((1, 3, d), lambda b, i: (b * per + i + skip, 0, 0)),
                  pl.BlockSpec((1, d), lambda b, i: (0, 0))],
        out_specs=pl.BlockSpec((1, r, d), lambda b, i: (b, i, 0)),
        out_shape=jax.ShapeDtypeStruct((bsz, seq, d), F32),
        name="final_residual",
        compiler_params=_cparams("parallel", "parallel"),
    )(x3, y3, modp, g_post.reshape(1, d))


A_TILE_BYTES = 12 * 1024 * 1024


def _mm_tile(a_ref, b_ref, cols=slice(None)):
    b = b_ref[0, :, cols] if len(b_ref.shape) == 3 else b_ref[:, cols]
    return jnp.dot(a_ref[...], b.astype(BF16), preferred_element_type=F32)


def _mm_kernel(a_ref, b_ref, o_ref):
    o_ref[...] = _mm_tile(a_ref, b_ref).astype(o_ref.dtype)


def _rope_cols(x, cos, sin, first):
    q4 = RET_QK_DIM // 4
    partner = jnp.where(first, pltpu.roll(x, RET_QK_DIM - q4, 1), pltpu.roll(x, q4, 1))
    return x * cos + partner * sin


def _proj_epilogue_kernel(*refs, segs, n16, has_tabs, has_plain):
    a_ref, b_ref = refs[0], refs[1]
    pos = 2
    if has_tabs:
        cos_ref, sin_ref = refs[2], refs[3]
        pos = 4
    o16_ref = refs[pos]
    j = pl.program_id(1)
    tn = o16_ref.shape[1]

    for lo, hi, mode, scale in segs:
        @pl.when(jnp.logical_and(j >= lo, j < hi))
        def _(mode=mode, scale=scale):
            if mode == "rope":
                cos, sin = cos_ref[...], sin_ref[...]
                lane = lax.broadcasted_iota(jnp.int32, cos.shape, 1)
                first = (lane % (RET_QK_DIM // 2)) < RET_QK_DIM // 4
            for grp in range(tn // MXU_COLS):
                acc = _mm_tile(a_ref, b_ref, slice(grp * MXU_COLS, (grp + 1) * MXU_COLS))
                if mode == "rope":
                    acc = jnp.concatenate(
                        [_rope_cols(acc[:, h * RET_QK_DIM:(h + 1) * RET_QK_DIM], cos, sin, first)
                         for h in range(MXU_COLS // RET_QK_DIM)], axis=1)
                    if scale != 1.0:
                        acc = acc * scale
                elif mode == "silu":
                    acc = _silu(acc)
                o16_ref[:, grp * MXU_COLS:(grp + 1) * MXU_COLS] = acc.astype(o16_ref.dtype)

    if has_plain:
        @pl.when(j >= n16)
        def _():
            refs[pos + 1][...] = _mm_tile(a_ref, b_ref)


def _proj_operands(a, b, tn, layer, col0, n):
    m, k = a.shape
    tm = next(c for c in (1536, 1024, 768, 512, 256)
              if m % c == 0 and c * k * a.dtype.itemsize <= A_TILE_BYTES)
    if layer is not None and (col0 % LANES or tn % LANES):
        b, layer = b[layer, :, col0:col0 + n].astype(BF16), None
    if layer is None:
        b_spec = pl.BlockSpec((k, tn), lambda i, j: (0, j))
    else:
        b_spec = pl.BlockSpec(
            (pl.Element(1), pl.Element(k), pl.Element(tn)),
            lambda i, j: (layer, 0, pl.multiple_of(col0 + j * tn, LANES)))
    return tm, b, b_spec


def _matmul(a, b, tn, out_dtype=F32, layer=None, col0=0, n=None):
    m, k = a.shape
    n = b.shape[-1] if n is None else n
    tm, b, b_spec = _proj_operands(a, b, tn, layer, col0, n)
    return pl.pallas_call(
        _mm_kernel,
        grid=(m // tm, n // tn),
        in_specs=[pl.BlockSpec((tm, k), lambda i, j: (i, 0)), b_spec],
        out_specs=pl.BlockSpec((tm, tn), lambda i, j: (i, j)),
        out_shape=jax.ShapeDtypeStruct((m, n), out_dtype),
        name="proj_k%d" % k,
        compiler_params=_cparams("parallel", "arbitrary"),
    )(a, b)


def _matmul_epilogue(a, b, tn, layer, col0, parts16, n_plain, tabs=None):
    m, k = a.shape
    n16 = sum(p[0] for p in parts16)
    tm, b, b_spec = _proj_operands(a, b, tn, layer, col0, n16 + n_plain)
    segs, lo = [], 0
    for n_cols, mode, scale in parts16:
        segs.append((lo, lo + n_cols // tn, mode, scale))
        lo += n_cols // tn
    t16 = lo
    in_specs = [pl.BlockSpec((tm, k), lambda i, j: (i, 0)), b_spec]
    args = [a, b]
    if tabs is not None:
        in_specs += [pl.BlockSpec((tm, RET_QK_DIM), lambda i, j: (i, 0))] * 2
        args += list(tabs)
    out_specs = [pl.BlockSpec((tm, tn), lambda i, j: (i, jnp.minimum(j, t16 - 1)))]
    out_shape = [jax.ShapeDtypeStruct((m, n16), BF16)]
    if n_plain:
        out_specs.append(pl.BlockSpec((tm, tn), lambda i, j: (i, jnp.maximum(j - t16, 0))))
        out_shape.append(jax.ShapeDtypeStruct((m, n_plain), F32))
    outs = pl.pallas_call(
        functools.partial(_proj_epilogue_kernel, segs=tuple(segs), n16=t16,
                          has_tabs=tabs is not None, has_plain=bool(n_plain)),
        grid=(m // tm, (n16 + n_plain) // tn),
        in_specs=in_specs,
        out_specs=out_specs,
        out_shape=out_shape,
        name="proj_epi_k%d" % k,
        compiler_params=_cparams("parallel", "arbitrary"),
    )(*args)
    return outs[0], (outs[1] if n_plain else None)


def _mm_acc_kernel(a_ref, b_ref, p_ref, o_ref):
    o_ref[...] = p_ref[...] + _mm_tile(a_ref, b_ref)


def _matmul_ksplit(a, b_all, layer, tn):
    m, k = a.shape
    n = b_all.shape[2]
    kh = k // 2
    tm = next(c for c in (1536, 1024, 768, 512, 256)
              if m % c == 0 and c * kh * a.dtype.itemsize <= A_TILE_BYTES)
    out = None
    for half in range(2):
        a_spec = pl.BlockSpec((tm, kh), lambda i, j, half=half: (i, half))
        b_spec = pl.BlockSpec(
            (pl.Element(1), pl.Element(kh), pl.Element(tn)),
            lambda i, j, half=half: (layer, half * kh, pl.multiple_of(j * tn, LANES)))
        o_spec = pl.BlockSpec((tm, tn), lambda i, j: (i, j))
        first = out is None
        out = pl.pallas_call(
            _mm_kernel if first else _mm_acc_kernel,
            grid=(m // tm, n // tn),
            in_specs=[a_spec, b_spec] + ([] if first else [o_spec]),
            out_specs=o_spec,
            out_shape=jax.ShapeDtypeStruct((m, n), F32),
            name="proj_half_k%d" % k,
            compiler_params=_cparams("parallel", "arbitrary"),
        )(*([a, b_all] if first else [a, b_all, out]))
    return out


def _slab_store(ref, row0, val):
    for k in range(ref.shape[0]):
        ref[k, row0:row0 + val.shape[0], :] = val[:, k * LANES:(k + 1) * LANES]


def _slab_load(ref, row0, rows):
    return jnp.concatenate([ref[k, pl.ds(row0, rows, stride=1), :] for k in range(ref.shape[0])],
                           axis=1)


def _conv_taps(x, prev8, next8, w_ref, b_ref, tile, n_tiles, pad_ref):
    r = x.shape[0]
    ctx_tiles = CTX_LEN // r
    prev_ok = jnp.logical_and(tile != 0, tile != ctx_tiles)
    next_ok = jnp.logical_and(tile != ctx_tiles - 1, tile != n_tiles - 1)
    _slab_store(pad_ref, 0, jnp.where(prev_ok, prev8, 0.0))
    _slab_store(pad_ref, 8, x)
    _slab_store(pad_ref, 8 + r, jnp.where(next_ok, next8, 0.0))
    return (b_ref[...] + w_ref[0:1, :] * _slab_load(pad_ref, 6, r)
            + w_ref[1:2, :] * _slab_load(pad_ref, 7, r)
            + w_ref[2:3, :] * x + w_ref[3:4, :] * _slab_load(pad_ref, 9, r))


def _halo_specs(r, cw, col_map, n_tiles):
    rb = r // 8
    last = n_tiles * rb - 1
    return [
        pl.BlockSpec((1, r, cw), lambda b, t, j: (b, t, col_map(j))),
        pl.BlockSpec((1, 8, cw), lambda b, t, j: (b, jnp.maximum(t * rb - 1, 0), col_map(j))),
        pl.BlockSpec((1, 8, cw), lambda b, t, j: (b, jnp.minimum((t + 1) * rb, last), col_map(j))),
    ]


def _conv_silu_kernel(x_ref, p_ref, n_ref, w_ref, b_ref, o_ref, pad_ref, *, n_tiles):
    y = _conv_taps(x_ref[0], p_ref[0], n_ref[0], w_ref, b_ref, pl.program_id(1), n_tiles, pad_ref)
    o_ref[0] = _silu(y).astype(o_ref.dtype)


def _conv_silu(pr, conv_w, conv_b, col_map, n_cols, cw):
    bsz, t, _ = pr.shape
    r = ROW_TILE
    n_tiles = t // r
    return pl.pallas_call(
        functools.partial(_conv_silu_kernel, n_tiles=n_tiles),
        grid=(bsz, n_tiles, n_cols // cw),
        in_specs=_halo_specs(r, cw, col_map, n_tiles) + [
            pl.BlockSpec((CONV_W, cw), lambda b, t, j: (0, j)),
            pl.BlockSpec((1, cw), lambda b, t, j: (0, j))],
        out_specs=pl.BlockSpec((1, r, cw), lambda b, t, j: (b, t, j)),
        out_shape=jax.ShapeDtypeStruct((bsz, t, n_cols), BF16),
        scratch_shapes=[pltpu.VMEM((cw // LANES, r + 16, LANES), F32)],
        name="conv_silu",
        compiler_params=_cparams("parallel", "parallel", "parallel"),
    )(pr, pr, pr, conv_w, conv_b.reshape(1, n_cols))


def _rope_tables(seq, bsz):
    rows = seq // GRID_W
    row = jnp.repeat(jnp.arange(rows), GRID_W).astype(F32)
    col = jnp.tile(jnp.arange(GRID_W), rows).astype(F32)
    q4 = RET_QK_DIM // 4
    freq = ROPE_BASE ** (-jnp.arange(q4, dtype=F32) / q4)
    ang = jnp.concatenate([row[:, None] * freq, row[:, None] * freq,
                           col[:, None] * freq, col[:, None] * freq], axis=1)
    first = (jnp.arange(RET_QK_DIM) % (2 * q4)) < q4
    cos, sin = jnp.cos(ang), jnp.sin(ang)
    sin = jnp.where(first, -sin, sin)
    pad = lambda tab, v: jnp.tile(
        jnp.concatenate([jnp.full((CTX_LEN, RET_QK_DIM), v, F32), tab], axis=0), (bsz, 1))
    return pad(cos, 1.0), pad(sin, 0.0)


def _chunk_order(step, n_ctx, n_all, reverse):
    if not reverse:
        return step
    return jnp.where(step < n_ctx, n_ctx - 1 - step, n_all - 1 + n_ctx - step)


def _tri_mask(reverse):
    i = lax.broadcasted_iota(jnp.int32, (CHUNK, CHUNK), 0)
    j = lax.broadcasted_iota(jnp.int32, (CHUNK, CHUNK), 1)
    return (j >= i) if reverse else (j <= i)


def _scan_chunks(n_rows, reverse):
    order = range(n_rows // CHUNK)
    return [slice(c * CHUNK, (c + 1) * CHUNK) for c in (reversed(order) if reverse else order)]


def _ssd_kernel(xbc_ref, dt_ref, bias_ref, alog_ref, y_ref, h_ref, *, reverse, d_ssd, heads):
    @pl.when(pl.program_id(1) == 0)
    def _():
        h_ref[...] = jnp.zeros_like(h_ref)

    for rows in _scan_chunks(y_ref.shape[1], reverse):
        _ssd_chunk(xbc_ref, dt_ref, bias_ref, alog_ref, y_ref, h_ref, rows, reverse, d_ssd, heads)


def _ssd_chunk(xbc_ref, dt_ref, bias_ref, alog_ref, y_ref, h_ref, rows, reverse, d_ssd, heads):
    hpg = heads // SSD_GROUPS
    gw = hpg * SSD_HEAD_DIM
    d = 1 if reverse else 0
    mask = _tri_mask(reverse)
    hpt = LANES // SSD_HEAD_DIM
    lane_head = lax.broadcasted_iota(jnp.int32, (CHUNK, LANES), 1) // SSD_HEAD_DIM
    lane_head_row = lax.broadcasted_iota(jnp.int32, (1, LANES), 1) // SSD_HEAD_DIM
    lo = d * heads
    dtv = _softplus(dt_ref[0, rows] + bias_ref[...])[:, lo:lo + heads]
    la = dtv * ((-1.0 / math.log(2.0)) * jnp.exp(alog_ref[...]))[:, lo:lo + heads]
    acs = jnp.dot(mask.astype(F32), la, precision=lax.Precision.HIGHEST,
                  preferred_element_type=F32)
    tot = acs[0:1, :] if reverse else acs[CHUNK - 1:CHUNK, :]
    e_tot = jnp.exp2(tot)
    w = jnp.exp2(tot - acs) * dtv
    acs_w_t = jnp.concatenate([acs, w], axis=1).T
    acs_t, w_t = acs_w_t[:heads], acs_w_t[heads:]
    dt_t = jnp.concatenate([dtv, dtv], axis=1).T[:heads]

    bm_off = d_ssd
    cm_off = d_ssd + SSD_GROUPS * SSD_STATE
    for g in range(SSD_GROUPS):
        cg = xbc_ref[0, rows, cm_off + g * SSD_STATE:cm_off + (g + 1) * SSD_STATE]
        bg = xbc_ref[0, rows, bm_off + g * SSD_STATE:bm_off + (g + 1) * SSD_STATE]
        scores = lax.dot_general(cg, bg, (((1,), (1,)), ((), ())), preferred_element_type=F32)
        cg32 = cg.astype(F32)
        bg_t = bg.astype(F32).T
        hg = h_ref[g]
        hg16 = hg.astype(BF16)
        xg = xbc_ref[0, rows, g * gw:(g + 1) * gw]
        for lt in range(gw // LANES):
            cols = slice(lt * LANES, (lt + 1) * LANES)
            xl, hl = xg[:, cols], hg16[:, cols]
            y = upd = dec = None
            for sub in range(hpt):
                hd = g * hpg + lt * hpt + sub
                col_b = jnp.broadcast_to(acs[:, hd:hd + 1], (CHUNK, CHUNK))
                dm = jnp.exp2(jnp.where(mask, col_b - acs_t[hd:hd + 1, :], NEG_BIG))
                p = (scores * dm * dt_t[hd:hd + 1, :]).astype(BF16)
                c_in = (cg32 * jnp.exp2(col_b)).astype(BF16)
                y_h = (jnp.dot(p, xl, preferred_element_type=F32)
                       + jnp.dot(c_in, hl, preferred_element_type=F32))
                b_out = (bg_t * w_t[hd:hd + 1, :]).astype(BF16)
                upd_h = jnp.dot(b_out, xl, preferred_element_type=F32)
                dec_h = jnp.broadcast_to(e_tot[:, hd:hd + 1], (1, LANES))
                mine = lane_head == sub
                y = y_h if y is None else jnp.where(mine, y_h, y)
                upd = upd_h if upd is None else jnp.where(mine, upd_h, upd)
                dec = dec_h if dec is None else jnp.where(lane_head_row == sub, dec_h, dec)
            y_ref[0, rows, g * gw + lt * LANES:g * gw + (lt + 1) * LANES] = y.astype(y_ref.dtype)
            h_ref[g, :, cols] = hg[:, cols] * dec + upd


def _ssd_scan(xbc, dt, dt_bias, a_log, reverse):
    bsz, t, _ = xbc.shape
    heads = dt.shape[2] // 2
    d_ssd = heads * SSD_HEAD_DIM
    nc, n_ctx = t // SCAN_ROWS, CTX_LEN // SCAN_ROWS
    order = lambda b, s: (b, _chunk_order(s, n_ctx, nc, reverse), 0)
    return pl.pallas_call(
        functools.partial(_ssd_kernel, reverse=reverse, d_ssd=d_ssd, heads=heads),
        grid=(bsz, nc),
        in_specs=[pl.BlockSpec((1, SCAN_ROWS, xbc.shape[2]), order),
                  pl.BlockSpec((1, SCAN_ROWS, 2 * heads), order),
                  pl.BlockSpec((1, 2 * heads), lambda b, s: (0, 0)),
                  pl.BlockSpec((1, 2 * heads), lambda b, s: (0, 0))],
        out_specs=pl.BlockSpec((1, SCAN_ROWS, d_ssd), order),
        out_shape=jax.ShapeDtypeStruct((bsz, t, d_ssd), BF16),
        scratch_shapes=[pltpu.VMEM((SSD_GROUPS, SSD_STATE, d_ssd // SSD_GROUPS), F32)],
        name="ssd_scan_bwd" if reverse else "ssd_scan_fwd",
        compiler_params=_cparams("parallel", "arbitrary"),
    )(xbc, dt, dt_bias.reshape(1, -1), a_log.reshape(1, -1))


def _ret_kernel(q_ref, k_ref, v_ref, y_ref, h_ref, dec_ref, *, reverse, v_dim):
    heads = range(RET_HEADS)
    lgs = [math.log1p(-(2.0 ** (-5.0 - (RET_HEADS - 1 - h if reverse else h)))) for h in heads]

    @pl.when(pl.program_id(1) == 0)
    def _():
        h_ref[...] = jnp.zeros_like(h_ref)
        mask = _tri_mask(reverse)
        i = lax.broadcasted_iota(jnp.int32, (CHUNK, CHUNK), 0)
        j = lax.broadcasted_iota(jnp.int32, (CHUNK, CHUNK), 1)
        dist = (jnp.abs(i - j)).astype(F32)
        n_in = ((CHUNK - i) if reverse else (i + 1)).astype(F32)
        n_out = (i if reverse else (CHUNK - 1 - i)).astype(F32)
        for h in heads:
            lg2 = lgs[h] / math.log(2.0)
            dec_ref[0, h] = jnp.where(mask, jnp.exp2(lg2 * dist), 0.0)
            dec_ref[1, h] = jnp.exp2(lg2 * n_in)
            dec_ref[2, h] = jnp.exp2(lg2 * n_out)

    for rows in _scan_chunks(y_ref.shape[1], reverse):
        _ret_chunk(q_ref, k_ref, v_ref, y_ref, h_ref, dec_ref, rows, lgs, v_dim)


def _ret_chunk(q_ref, k_ref, v_ref, y_ref, h_ref, dec_ref, rows, lgs, v_dim):
    heads = range(RET_HEADS)
    reps = v_dim // CHUNK
    qs = [q_ref[0, rows, h * RET_QK_DIM:(h + 1) * RET_QK_DIM] for h in heads]
    ks = [k_ref[0, rows, h * RET_QK_DIM:(h + 1) * RET_QK_DIM] for h in heads]
    vs = [v_ref[0, rows, h * v_dim:(h + 1) * v_dim].astype(BF16) for h in heads]
    hs = [h_ref[h] for h in heads]
    scores = [lax.dot_general(qs[h], ks[h], (((1,), (1,)), ((), ())),
                              preferred_element_type=F32) for h in heads]
    inter = [jnp.dot(qs[h], hs[h].astype(BF16), preferred_element_type=F32) for h in heads]
    ps = [(scores[h] * dec_ref[0, h]).astype(BF16) for h in heads]
    k_out = [(ks[h].astype(F32) * dec_ref[2, h]).astype(BF16) for h in heads]
    intra = [jnp.dot(ps[h], vs[h], preferred_element_type=F32) for h in heads]
    upd = [lax.dot_general(k_out[h], vs[h], (((0,), (0,)), ((), ())),
                           preferred_element_type=F32) for h in heads]
    for h in heads:
        e_in = jnp.concatenate([dec_ref[1, h]] * reps, axis=1)
        y_ref[0, rows, h * v_dim:(h + 1) * v_dim] = (intra[h] + e_in * inter[h]).astype(y_ref.dtype)
    for h in heads:
        h_ref[h] = hs[h] * math.exp(lgs[h] * CHUNK) + upd[h]


def _ret_scan(qkv, v_blk, v_width, reverse):
    qk = pr = qkv
    bsz, t, _ = qkv.shape
    qk_width = RET_HEADS * RET_QK_DIM
    v_dim = v_width // RET_HEADS
    nc, n_ctx = t // SCAN_ROWS, CTX_LEN // SCAN_ROWS
    order = lambda b, s: (b, _chunk_order(s, n_ctx, nc, reverse), 0)
    col = lambda c: (lambda b, s: (b, _chunk_order(s, n_ctx, nc, reverse), c))
    return pl.pallas_call(
        functools.partial(_ret_kernel, reverse=reverse, v_dim=v_dim),
        grid=(bsz, nc),
        in_specs=[pl.BlockSpec((1, SCAN_ROWS, qk_width), col(0)),
                  pl.BlockSpec((1, SCAN_ROWS, qk_width), col(1)),
                  pl.BlockSpec((1, SCAN_ROWS, v_width), col(v_blk))],
        out_specs=pl.BlockSpec((1, SCAN_ROWS, v_width), order),
        out_shape=jax.ShapeDtypeStruct((bsz, t, v_width), BF16),
        scratch_shapes=[pltpu.VMEM((RET_HEADS, RET_QK_DIM, v_dim), F32),
                        pltpu.VMEM((3, RET_HEADS, CHUNK, CHUNK), F32)],
        name="ret_scan_bwd" if reverse else "ret_scan_fwd",
        compiler_params=_cparams("parallel", "arbitrary"),
    )(qk, qk, pr)


def _even_out_kernel(ysf_ref, ysb_ref, yrf_ref, yrb_ref, xs_ref, z_ref, g_ref,
                     dsk_ref, sn_ref, rn_ref, o_ref, *, d_ssd, v_dim):
    ys = (ysf_ref[0].astype(F32) + ysb_ref[0].astype(F32)
          + dsk_ref[...] * xs_ref[0].astype(F32))
    ys = ys * z_ref[0]
    ys = ys * lax.rsqrt(jnp.mean(ys * ys, axis=-1, keepdims=True) + EPS) * sn_ref[...]
    o_ref[0, :, :d_ssd] = ys.astype(o_ref.dtype)
    for h in range(RET_HEADS):
        cols = slice(h * v_dim, (h + 1) * v_dim)
        yr = yrf_ref[0, :, cols].astype(F32) + yrb_ref[0, :, cols].astype(F32)
        mu = jnp.mean(yr, axis=-1, keepdims=True)
        dev = yr - mu
        var = jnp.mean(dev * dev, axis=-1, keepdims=True)
        yn = dev * lax.rsqrt(var + EPS) * rn_ref[:, cols]
        o_ref[0, :, d_ssd + h * v_dim:d_ssd + (h + 1) * v_dim] = (
            yn * g_ref[0, :, cols]).astype(o_ref.dtype)


def _even_out(ysf, ysb, yrf, yrb, xbc, pa, pb, z_blk, g_blk, d_skip_row, ssd_norm, ret_norm):
    bsz, t, d_ssd = ysf.shape
    d_ret = yrf.shape[2]
    r = ROW_TILE
    blk = lambda w, c: pl.BlockSpec((1, r, w), lambda b, i: (b, i, c))
    row = lambda w: pl.BlockSpec((1, w), lambda b, i: (0, 0))
    return pl.pallas_call(
        functools.partial(_even_out_kernel, d_ssd=d_ssd, v_dim=d_ret // RET_HEADS),
        grid=(bsz, t // r),
        in_specs=[blk(d_ssd, 0), blk(d_ssd, 0), blk(d_ret, 0), blk(d_ret, 0),
                  blk(d_ssd, 0), blk(d_ssd, z_blk), blk(d_ret, g_blk),
                  row(d_ssd), row(d_ssd), row(d_ret)],
        out_specs=blk(d_ssd + d_ret, 0),
        out_shape=jax.ShapeDtypeStruct((bsz, t, d_ssd + d_ret), BF16),
        name="even_out",
        compiler_params=_cparams("parallel", "parallel"),
    )(ysf, ysb, yrf, yrb, xbc, pa, pb, d_skip_row, ssd_norm.reshape(1, -1),
      ret_norm.reshape(1, -1))


def _lru_kernel(*refs, reverse, n_tiles, final):
    if final:
        (x_ref, p_ref, n_ref, cw_ref, cb_ref, gw_ref, gb_ref, lam_ref, yf_ref, gate_ref,
         o_ref, carry_ref, pad_ref, as_ref, us_ref) = refs
    else:
        (x_ref, p_ref, n_ref, cw_ref, cb_ref, gw_ref, gb_ref, lam_ref,
         o_ref, carry_ref, pad_ref, as_ref, us_ref) = refs
    step = pl.program_id(2)
    tile = _chunk_order(step, CTX_LEN // ROW_TILE, n_tiles, reverse)

    @pl.when(step == 0)
    def _():
        carry_ref[...] = jnp.zeros_like(carry_ref)

    xf = _conv_taps(x_ref[0], p_ref[0], n_ref[0], cw_ref, cb_ref, tile, n_tiles, pad_ref)
    r_rows, cb = xf.shape
    xb = xf.astype(BF16)
    rate = (-LRU_C / math.log(2.0)) * _softplus(-lam_ref[0])
    a_parts, u_parts = [], []
    for kb in range(cb // RNN_BLOCK):
        cols = slice(kb * RNN_BLOCK, (kb + 1) * RNN_BLOCK)
        xk = xb[:, cols]
        rg = jax.nn.sigmoid(jnp.dot(xk, gw_ref[0, 0, kb], preferred_element_type=F32)
                            + gb_ref[0, 0:1, cols])
        ig = jax.nn.sigmoid(jnp.dot(xk, gw_ref[0, 1, kb], preferred_element_type=F32)
                            + gb_ref[0, 1:2, cols])
        a = jnp.exp2(rate[:, cols] * rg)
        a_parts.append(a)
        s = 1.0 - a * a
        root = s * lax.rsqrt(jnp.maximum(s, jnp.finfo(F32).tiny))
        u_parts.append(root * (ig * xf[:, cols]))
    a = jnp.concatenate(a_parts, axis=1)
    u = jnp.concatenate(u_parts, axis=1)

    body = 0 if reverse else 8
    pad = r_rows if reverse else 0
    _slab_store(as_ref, pad, jnp.ones((8, cb), F32))
    _slab_store(us_ref, pad, jnp.zeros((8, cb), F32))
    for s in (1, 2, 4):
        _slab_store(as_ref, body, a)
        _slab_store(us_ref, body, u)
        src = body + s if reverse else body - s
        u = a * _slab_load(us_ref, src, r_rows) + u
        a = a * _slab_load(as_ref, src, r_rows)
    h = carry_ref[...]
    n_grp = r_rows // 8
    outs = [None] * n_grp
    for v in (range(n_grp - 1, -1, -1) if reverse else range(n_grp)):
        h = a[v * 8:(v + 1) * 8] * h + u[v * 8:(v + 1) * 8]
        outs[v] = h
    last = h[0:1] if reverse else h[7:8]
    carry_ref[...] = jnp.broadcast_to(last, carry_ref.shape)
    y = jnp.concatenate(outs, axis=0)
    if final:
        o_ref[0] = ((yf_ref[0] + y) * gate_ref[0]).astype(o_ref.dtype)
    else:
        o_ref[0] = y


def _lru_scan(pr, conv_w, conv_b, gate_w, gate_b, lam, reverse, yf=None, gate=None):
    bsz, t, _ = pr.shape
    x_blk0 = gate_blk0 = 0
    d_rnn = conv_w.shape[1]
    cb = min(COL_TILE, d_rnn)
    r = ROW_TILE
    n_tiles = t // r
    d = 1 if reverse else 0
    final = yf is not None
    tile_of = lambda s: _chunk_order(s, CTX_LEN // r, n_tiles, reverse)
    rb = r // 8
    last = n_tiles * rb - 1
    in_specs = [
        pl.BlockSpec((1, r, cb), lambda b, j, s: (b, tile_of(s), x_blk0 + j)),
        pl.BlockSpec((1, 8, cb), lambda b, j, s: (b, jnp.maximum(tile_of(s) * rb - 1, 0), x_blk0 + j)),
        pl.BlockSpec((1, 8, cb), lambda b, j, s: (b, jnp.minimum((tile_of(s) + 1) * rb, last), x_blk0 + j)),
        pl.BlockSpec((CONV_W, cb), lambda b, j, s: (0, j)),
        pl.BlockSpec((1, cb), lambda b, j, s: (0, j)),
        pl.BlockSpec((1, 2, cb // RNN_BLOCK, RNN_BLOCK, RNN_BLOCK), lambda b, j, s: (d, 0, j, 0, 0)),
        pl.BlockSpec((1, 2, cb), lambda b, j, s: (d, 0, j)),
        pl.BlockSpec((1, 1, cb), lambda b, j, s: (d, 0, j)),
    ]
    args = [pr, pr, pr, conv_w, conv_b.reshape(1, d_rnn), gate_w, gate_b, lam.reshape(2, 1, d_rnn)]
    if final:
        in_specs += [pl.BlockSpec((1, r, cb), lambda b, j, s: (b, tile_of(s), j)),
                     pl.BlockSpec((1, r, cb), lambda b, j, s: (b, tile_of(s), gate_blk0 + j))]
        args += [yf, gate]
    return pl.pallas_call(
        functools.partial(_lru_kernel, reverse=reverse, n_tiles=n_tiles, final=final),
        grid=(bsz, d_rnn // cb, n_tiles),
        in_specs=in_specs,
        out_specs=pl.BlockSpec((1, r, cb), lambda b, j, s: (b, tile_of(s), j)),
        out_shape=jax.ShapeDtypeStruct((bsz, t, d_rnn), BF16 if final else F32),
        scratch_shapes=[pltpu.VMEM((8, cb), F32), pltpu.VMEM((cb // LANES, r + 16, LANES), F32),
                        pltpu.VMEM((cb // LANES, r + 8, LANES), F32),
                        pltpu.VMEM((cb // LANES, r + 8, LANES), F32)],
        name="lru_scan_bwd" if reverse else "lru_scan_fwd",
        compiler_params=_cparams("parallel", "parallel", "arbitrary"),
    )(*args)


def _even_layer(h2, bsz, t, w_in_all, layer, conv_w, conv_b, dt_bias, a_log, d_skip, ssd_norm,
                ret_norm, rope_tabs):
    d_model = h2.shape[1]
    d_ssd = d_model
    d_ret = d_model
    gn = SSD_GROUPS * SSD_STATE
    heads = d_ssd // SSD_HEAD_DIM
    qk = RET_HEADS * RET_QK_DIM
    n_a, n_dt = 2 * d_ssd + 2 * gn, 2 * heads
    n_b = w_in_all.shape[2] - n_a - n_dt
    assert n_b == 2 * qk + 2 * d_ret
    zs, xbc_pre = _matmul_epilogue(h2, w_in_all, 512, layer, 0, [(d_ssd, "silu", 1.0)],
                                   d_ssd + 2 * gn)
    qkvg, _ = _matmul_epilogue(
        h2, w_in_all, 512, layer, n_a + n_dt,
        [(qk, "rope", 1.0), (qk, "rope", RET_QK_DIM ** -0.5), (d_ret, "cast", 1.0),
         (d_ret, "silu", 1.0)], 0, tabs=rope_tabs)
    zs = zs.reshape(bsz, t, d_ssd)
    xbc_pre = xbc_pre.reshape(bsz, t, -1)
    qkvg = qkvg.reshape(bsz, t, -1)
    dt = _matmul(h2, w_in_all, n_dt, layer=layer, col0=n_a, n=n_dt).reshape(bsz, t, n_dt)
    v_blk = 2 * qk // d_ret
    g_blk = v_blk + 1

    xbc = _conv_silu(xbc_pre, conv_w, conv_b, lambda j: j, d_ssd + 2 * gn, d_ssd + 2 * gn)

    ysf = _ssd_scan(xbc, dt, dt_bias, a_log, False)
    ysb = _ssd_scan(xbc, dt, dt_bias, a_log, True)
    yrf = _ret_scan(qkvg, v_blk, d_ret, False)
    yrb = _ret_scan(qkvg, v_blk, d_ret, True)
    d_skip_row = jnp.repeat(d_skip, SSD_HEAD_DIM).reshape(1, d_ssd)
    return _even_out(ysf, ysb, yrf, yrb, xbc, zs, qkvg, 0, g_blk, d_skip_row, ssd_norm,
                     ret_norm)


def _odd_layer(h2, bsz, t, w_in_all, layer, conv_w, conv_b, gate_w, gate_b, lam):
    d_rnn = conv_w.shape[1]
    gate, xr = _matmul_epilogue(h2, w_in_all, 512, layer, 0, [(d_rnn, "silu", 1.0)], d_rnn)
    gate = gate.reshape(bsz, t, d_rnn)
    xr = xr.reshape(bsz, t, d_rnn)
    gw = gate_w.astype(BF16)
    yf = _lru_scan(xr, conv_w, conv_b, gw, gate_b, lam, False)
    return _lru_scan(xr, conv_w, conv_b, gw, gate_b, lam, True, yf=yf, gate=gate)


def kernel(x, c, ctx, c_ctx, ada_w, ada_b, norm_pre, norm_post, e_w_in, e_conv_w, e_conv_b,
           e_dt_bias, e_a_log, e_d_skip, e_ssd_norm, e_ret_norm, e_w_out, o_w_in, o_conv_w,
           o_conv_b, o_gate_w, o_gate_b, o_lambda, o_w_out):
    bsz, seq, d = x.shape
    depth = ada_w.shape[0]
    t = CTX_LEN + seq
    m = bsz * t
    mod = _modulation(jnp.concatenate([c, c_ctx[None]], axis=0), ada_w, ada_b)
    n_ctx_blk, n_lat_blk = CTX_LEN // ROW_TILE, seq // ROW_TILE
    mod_lat = jnp.broadcast_to(mod[:, :bsz].reshape(depth, bsz, 1, 3, d),
                               (depth, bsz, n_lat_blk, 3, d))
    mod_ctx = jnp.broadcast_to(mod[:, bsz].reshape(depth, 1, 1, 3, d),
                               (depth, bsz, n_ctx_blk, 3, d))
    modblk = jnp.concatenate([mod_ctx, mod_lat], axis=2).reshape(depth, m // ROW_TILE, 3, d)

    rope_tabs = _rope_tables(seq, bsz)
    xs, h2 = _prenorm(ctx, x, norm_pre[0], modblk[0])
    xs, h2 = xs.reshape(m, d), h2.reshape(m, d)
    for i in range(depth):
        j = i // 2
        if i % 2 == 0:
            y = _even_layer(h2, bsz, t, e_w_in, j, e_conv_w[j], e_conv_b[j], e_dt_bias[j],
                            e_a_log[j], e_d_skip[j], e_ssd_norm[j], e_ret_norm[j], rope_tabs)
            w_out = e_w_out
        else:
            y = _odd_layer(h2, bsz, t, o_w_in, j, o_conv_w[j], o_conv_b[j], o_gate_w[j],
                           o_gate_b[j], o_lambda[j])
            w_out = o_w_out
        out = _matmul_ksplit(y.reshape(m, -1), w_out, j, 512)
        if i + 1 < depth:
            xs, h2 = _resnorm(xs, out, modblk[i], norm_post[i], norm_pre[i + 1], modblk[i + 1])
    return _final_residual(xs.reshape(bsz, t, d), out.reshape(bsz, t, d), modblk[depth - 1],
                           norm_post[depth - 1], seq)
```

```python
import functools
import math

import numpy as np
import jax
import jax.numpy as jnp
from jax import lax
from jax.experimental import pallas as pl
from jax.experimental.pallas import tpu as pltpu

F32 = jnp.float32
BF16 = jnp.bfloat16

GRID_W = 64
CTX_LEN = 256
SSD_HEAD_DIM = 64
SSD_GROUPS = 8
SSD_STATE = 128
RET_HEADS = 16
RET_QK_DIM = 128
RNN_BLOCK = 256
LRU_C = 8.0
CONV_W = 4
CHUNK = 128
ROPE_BASE = 10000.0
EPS = 1e-6

LANES = 128
MXU_COLS = 256
ROW_TILE = 256
COL_TILE = 2048
SCAN_ROWS = 2 * CHUNK
VMEM_LIMIT = 56 * 1024 * 1024
NEG_BIG = -1e30


def _cparams(*sem):
    return pltpu.CompilerParams(dimension_semantics=sem, vmem_limit_bytes=VMEM_LIMIT)


def _silu(x):
    return x * jax.nn.sigmoid(x)


def _softplus(x):
    return jnp.maximum(x, 0.0) + jnp.log1p(jnp.exp(-jnp.abs(x)))


def _mod_kernel(ct_ref, w_ref, b_ref, o_ref, *, n_vec):
    w = w_ref[0]
    o_ref[0] = jnp.zeros(o_ref.shape[1:], F32)
    for m in range(n_vec):
        col = _silu(ct_ref[:, m:m + 1])
        o_ref[0, m:m + 1, :] = jnp.sum(w * col, axis=0, keepdims=True) + b_ref[0]


def _modulation(cvecs, ada_w, ada_b):
    depth, d, n3 = ada_w.shape
    n_vec = cvecs.shape[0]
    ct = jnp.zeros((d, 8), F32).at[:, :n_vec].set(cvecs.T)
    tn = 1024
    return pl.pallas_call(
        functools.partial(_mod_kernel, n_vec=n_vec),
        grid=(depth, n3 // tn),
        in_specs=[pl.BlockSpec((d, 8), lambda i, j: (0, 0)),
                  pl.BlockSpec((1, d, tn), lambda i, j: (i, 0, j)),
                  pl.BlockSpec((1, 1, tn), lambda i, j: (i, 0, j))],
        out_specs=pl.BlockSpec((1, 8, tn), lambda i, j: (i, 0, j)),
        out_shape=jax.ShapeDtypeStruct((depth, 8, n3), F32),
        name="modulation",
        compiler_params=_cparams("parallel", "parallel"),
    )(ct, ada_w, ada_b.reshape(depth, 1, n3))


def _prenorm_kernel(ctx_ref, x_ref, g_ref, mod_ref, xs_ref, h_ref, *, ctx_tiles):
    x = jnp.where(pl.program_id(1) < ctx_tiles, ctx_ref[0], x_ref[0])
    xs_ref[0] = x
    y = x * lax.rsqrt(jnp.mean(x * x, axis=-1, keepdims=True) + EPS) * g_ref[...]
    h_ref[0] = (y * (1.0 + mod_ref[0, 1:2, :]) + mod_ref[0, 0:1, :]).astype(h_ref.dtype)


def _prenorm(ctx, x, g, modblk):
    bsz, seq, d = x.shape
    r = ROW_TILE
    ctx_tiles = ctx.shape[1] // r
    n_tiles = ctx_tiles + seq // r
    tile = pl.BlockSpec((1, r, d), lambda b, i: (b, i, 0))
    return pl.pallas_call(
        functools.partial(_prenorm_kernel, ctx_tiles=ctx_tiles),
        grid=(bsz, n_tiles),
        in_specs=[pl.BlockSpec((1, r, d), lambda b, i: (b, jnp.minimum(i, ctx_tiles - 1), 0)),
                  pl.BlockSpec((1, r, d), lambda b, i: (b, jnp.maximum(i - ctx_tiles, 0), 0)),
                  pl.BlockSpec((1, d), lambda b, i: (0, 0)),
                  pl.BlockSpec((1, 3, d), lambda b, i: (b * n_tiles + i, 0, 0))],
        out_specs=[tile, tile],
        out_shape=[jax.ShapeDtypeStruct((bsz, n_tiles * r, d), F32),
                   jax.ShapeDtypeStruct((bsz, n_tiles * r, d), BF16)],
        name="prenorm",
        compiler_params=_cparams("parallel", "parallel"),
    )(ctx, x, g.reshape(1, d), modblk)


def _residual(x, y, gate, g_post):
    return x + gate * (y * lax.rsqrt(jnp.mean(y * y, axis=-1, keepdims=True) + EPS) * g_post)


def _resnorm_kernel(x_ref, y_ref, modp_ref, gpost_ref, gpre_ref, modn_ref, xn_ref, h_ref):
    xn = _residual(x_ref[...], y_ref[...], modp_ref[0, 2:3, :], gpost_ref[...])
    xn_ref[...] = xn
    hn = xn * lax.rsqrt(jnp.mean(xn * xn, axis=-1, keepdims=True) + EPS) * gpre_ref[...]
    h_ref[...] = (hn * (1.0 + modn_ref[0, 1:2, :]) + modn_ref[0, 0:1, :]).astype(h_ref.dtype)


def _resnorm(x2, y2, modp, g_post, g_pre, modn):
    m, d = x2.shape
    r = ROW_TILE
    tile = pl.BlockSpec((r, d), lambda i: (i, 0))
    vec = pl.BlockSpec((1, d), lambda i: (0, 0))
    mod = pl.BlockSpec((1, 3, d), lambda i: (i, 0, 0))
    return pl.pallas_call(
        _resnorm_kernel,
        grid=(m // r,),
        in_specs=[tile, tile, mod, vec, vec, mod],
        out_specs=[tile, tile],
        out_shape=[jax.ShapeDtypeStruct((m, d), F32), jax.ShapeDtypeStruct((m, d), BF16)],
        name="resnorm",
        compiler_params=_cparams("parallel"),
    )(x2, y2, modp, g_post.reshape(1, d), g_pre.reshape(1, d), modn)


def _final_kernel(x_ref, y_ref, modp_ref, gpost_ref, o_ref):
    o_ref[0] = _residual(x_ref[0], y_ref[0], modp_ref[0, 2:3, :], gpost_ref[...])


def _final_residual(x3, y3, modp, g_post, seq):
    bsz, t, d = x3.shape
    r = ROW_TILE
    skip, per = (t - seq) // r, t // r
    tile = pl.BlockSpec((1, r, d), lambda b, i: (b, i + skip, 0))
    return pl.pallas_call(
        _final_kernel,
        grid=(bsz, seq // r),
        in_specs=[tile, tile,
                  pl.BlockSpec((1, 3, d), lambda b, i: (b * per + i + skip, 0, 0)),
                  pl.BlockSpec((1, d), lambda b, i: (0, 0))],
        out_specs=pl.BlockSpec((1, r, d), lambda b, i: (b, i, 0)),
        out_shape=jax.ShapeDtypeStruct((bsz, seq, d), F32),
        name="final_residual",
        compiler_params=_cparams("parallel", "parallel"),
    )(x3, y3, modp, g_post.reshape(1, d))


A_TILE_BYTES = 12 * 1024 * 1024


def _mm_tile(a_ref, b_ref, cols=slice(None)):
    b = b_ref[0, :, cols] if len(b_ref.shape) == 3 else b_ref[:, cols]
    return jnp.dot(a_ref[...], b.astype(BF16), preferred_element_type=F32)


def _mm_kernel(a_ref, b_ref, o_ref):
    o_ref[...] = _mm_tile(a_ref, b_ref).astype(o_ref.dtype)


def _rope_cols(x, cos, sin, first):
    q4 = RET_QK_DIM // 4
    partner = jnp.where(first, pltpu.roll(x, RET_QK_DIM - q4, 1), pltpu.roll(x, q4, 1))
    return x * cos + partner * sin


def _proj_epilogue_kernel(*refs, segs, n16, has_tabs, has_plain):
    a_ref, b_ref = refs[0], refs[1]
    pos = 2
    if has_tabs:
        cos_ref, sin_ref = refs[2], refs[3]
        pos = 4
    o16_ref = refs[pos]
    j = pl.program_id(1)
    tn = o16_ref.shape[1]

    for lo, hi, mode, scale in segs:
        @pl.when(jnp.logical_and(j >= lo, j < hi))
        def _(mode=mode, scale=scale):
            if mode == "rope":
                cos, sin = cos_ref[...], sin_ref[...]
                lane = lax.broadcasted_iota(jnp.int32, cos.shape, 1)
                first = (lane % (RET_QK_DIM // 2)) < RET_QK_DIM // 4
            for grp in range(tn // MXU_COLS):
                acc = _mm_tile(a_ref, b_ref, slice(grp * MXU_COLS, (grp + 1) * MXU_COLS))
                if mode == "rope":
                    acc = jnp.concatenate(
                        [_rope_cols(acc[:, h * RET_QK_DIM:(h + 1) * RET_QK_DIM], cos, sin, first)
                         for h in range(MXU_COLS // RET_QK_DIM)], axis=1)
                    if scale != 1.0:
                        acc = acc * scale
                elif mode == "silu":
                    acc = _silu(acc)
                o16_ref[:, grp * MXU_COLS:(grp + 1) * MXU_COLS] = acc.astype(o16_ref.dtype)

    if has_plain:
        @pl.when(j >= n16)
        def _():
            refs[pos + 1][...] = _mm_tile(a_ref, b_ref)


def _proj_operands(a, b, tn, layer, col0, n):
    m, k = a.shape
    tm = next(c for c in (1536, 1024, 768, 512, 256)
              if m % c == 0 and c * k * a.dtype.itemsize <= A_TILE_BYTES)
    if layer is not None and (col0 % LANES or tn % LANES):
        b, layer = b[layer, :, col0:col0 + n].astype(BF16), None
    if layer is None:
        b_spec = pl.BlockSpec((k, tn), lambda i, j: (0, j))
    else:
        b_spec = pl.BlockSpec(
            (pl.Element(1), pl.Element(k), pl.Element(tn)),
            lambda i, j: (layer, 0, pl.multiple_of(col0 + j * tn, LANES)))
    return tm, b, b_spec


def _matmul(a, b, tn, out_dtype=F32, layer=None, col0=0, n=None):
    m, k = a.shape
    n = b.shape[-1] if n is None else n
    tm, b, b_spec = _proj_operands(a, b, tn, layer, col0, n)
    return pl.pallas_call(
        _mm_kernel,
        grid=(m // tm, n // tn),
        in_specs=[pl.BlockSpec((tm, k), lambda i, j: (i, 0)), b_spec],
        out_specs=pl.BlockSpec((tm, tn), lambda i, j: (i, j)),
        out_shape=jax.ShapeDtypeStruct((m, n), out_dtype),
        name="proj_k%d" % k,
        compiler_params=_cparams("parallel", "arbitrary"),
    )(a, b)


def _matmul_epilogue(a, b, tn, layer, col0, parts16, n_plain, tabs=None):
    m, k = a.shape
    n16 = sum(p[0] for p in parts16)
    tm, b, b_spec = _proj_operands(a, b, tn, layer, col0, n16 + n_plain)
    segs, lo = [], 0
    for n_cols, mode, scale in parts16:
        segs.append((lo, lo + n_cols // tn, mode, scale))
        lo += n_cols // tn
    t16 = lo
    in_specs = [pl.BlockSpec((tm, k), lambda i, j: (i, 0)), b_spec]
    args = [a, b]
    if tabs is not None:
        in_specs += [pl.BlockSpec((tm, RET_QK_DIM), lambda i, j: (i, 0))] * 2
        args += list(tabs)
    out_specs = [pl.BlockSpec((tm, tn), lambda i, j: (i, jnp.minimum(j, t16 - 1)))]
    out_shape = [jax.ShapeDtypeStruct((m, n16), BF16)]
    if n_plain:
        out_specs.append(pl.BlockSpec((tm, tn), lambda i, j: (i, jnp.maximum(j - t16, 0))))
        out_shape.append(jax.ShapeDtypeStruct((m, n_plain), F32))
    outs = pl.pallas_call(
        functools.partial(_proj_epilogue_kernel, segs=tuple(segs), n16=t16,
                          has_tabs=tabs is not None, has_plain=bool(n_plain)),
        grid=(m // tm, (n16 + n_plain) // tn),
        in_specs=in_specs,
        out_specs=out_specs,
        out_shape=out_shape,
        name="proj_epi_k%d" % k,
        compiler_params=_cparams("parallel", "arbitrary"),
    )(*args)
    return outs[0], (outs[1] if n_plain else None)


def _mm_acc_kernel(a_ref, b_ref, p_ref, o_ref):
    o_ref[...] = p_ref[...] + _mm_tile(a_ref, b_ref)


def _matmul_ksplit(a, b_all, layer, tn):
    m, k = a.shape
    n = b_all.shape[2]
    kh = k // 2
    tm = next(c for c in (1536, 1024, 768, 512, 256)
              if m % c == 0 and c * kh * a.dtype.itemsize <= A_TILE_BYTES)
    out = None
    for half in range(2):
        a_spec = pl.BlockSpec((tm, kh), lambda i, j, half=half: (i, half))
        b_spec = pl.BlockSpec(
            (pl.Element(1), pl.Element(kh), pl.Element(tn)),
            lambda i, j, half=half: (layer, half * kh, pl.multiple_of(j * tn, LANES)))
        o_spec = pl.BlockSpec((tm, tn), lambda i, j: (i, j))
        first = out is None
        out = pl.pallas_call(
            _mm_kernel if first else _mm_acc_kernel,
            grid=(m // tm, n // tn),
            in_specs=[a_spec, b_spec] + ([] if first else [o_spec]),
            out_specs=o_spec,
            out_shape=jax.ShapeDtypeStruct((m, n), F32),
            name="proj_half_k%d" % k,
            compiler_params=_cparams("parallel", "arbitrary"),
        )(*([a, b_all] if first else [a, b_all, out]))
    return out


def _slab_store(ref, row0, val, k0=0):
    for k in range(val.shape[1] // LANES):
        ref[k0 + k, row0:row0 + val.shape[0], :] = val[:, k * LANES:(k + 1) * LANES]


def _slab_load(ref, row0, rows, k0=0, n=None):
    n = ref.shape[0] if n is None else n
    return jnp.concatenate([ref[k0 + k, pl.ds(row0, rows, stride=1), :] for k in range(n)],
                           axis=1)


def _conv_taps(x, prev8, next8, w_ref, b_ref, tile, n_tiles, pad_ref):
    r = x.shape[0]
    ctx_tiles = CTX_LEN // r
    prev_ok = jnp.logical_and(tile != 0, tile != ctx_tiles)
    next_ok = jnp.logical_and(tile != ctx_tiles - 1, tile != n_tiles - 1)
    _slab_store(pad_ref, 0, jnp.where(prev_ok, prev8, 0.0))
    _slab_store(pad_ref, 8, x)
    _slab_store(pad_ref, 8 + r, jnp.where(next_ok, next8, 0.0))
    return (b_ref[...] + w_ref[0:1, :] * _slab_load(pad_ref, 6, r)
            + w_ref[1:2, :] * _slab_load(pad_ref, 7, r)
            + w_ref[2:3, :] * x + w_ref[3:4, :] * _slab_load(pad_ref, 9, r))


def _halo_specs(r, cw, col_map, n_tiles):
    rb = r // 8
    last = n_tiles * rb - 1
    return [
        pl.BlockSpec((1, r, cw), lambda b, t, j: (b, t, col_map(j))),
        pl.BlockSpec((1, 8, cw), lambda b, t, j: (b, jnp.maximum(t * rb - 1, 0), col_map(j))),
        pl.BlockSpec((1, 8, cw), lambda b, t, j: (b, jnp.minimum((t + 1) * rb, last), col_map(j))),
    ]


def _conv_silu_kernel(x_ref, p_ref, n_ref, w_ref, b_ref, o_ref, pad_ref, *, n_tiles):
    y = _conv_taps(x_ref[0], p_ref[0], n_ref[0], w_ref, b_ref, pl.program_id(1), n_tiles, pad_ref)
    o_ref[0] = _silu(y).astype(o_ref.dtype)


def _conv_silu(pr, conv_w, conv_b, col_map, n_cols, cw):
    bsz, t, _ = pr.shape
    r = ROW_TILE
    n_tiles = t // r
    return pl.pallas_call(
        functools.partial(_conv_silu_kernel, n_tiles=n_tiles),
        grid=(bsz, n_tiles, n_cols // cw),
        in_specs=_halo_specs(r, cw, col_map, n_tiles) + [
            pl.BlockSpec((CONV_W, cw), lambda b, t, j: (0, j)),
            pl.BlockSpec((1, cw), lambda b, t, j: (0, j))],
        out_specs=pl.BlockSpec((1, r, cw), lambda b, t, j: (b, t, j)),
        out_shape=jax.ShapeDtypeStruct((bsz, t, n_cols), BF16),
        scratch_shapes=[pltpu.VMEM((cw // LANES, r + 16, LANES), F32)],
        name="conv_silu",
        compiler_params=_cparams("parallel", "parallel", "parallel"),
    )(pr, pr, pr, conv_w, conv_b.reshape(1, n_cols))


def _rope_tables(seq, bsz):
    rows = seq // GRID_W
    row = jnp.repeat(jnp.arange(rows), GRID_W).astype(F32)
    col = jnp.tile(jnp.arange(GRID_W), rows).astype(F32)
    q4 = RET_QK_DIM // 4
    freq = ROPE_BASE ** (-jnp.arange(q4, dtype=F32) / q4)
    ang = jnp.concatenate([row[:, None] * freq, row[:, None] * freq,
                           col[:, None] * freq, col[:, None] * freq], axis=1)
    first = (jnp.arange(RET_QK_DIM) % (2 * q4)) < q4
    cos, sin = jnp.cos(ang), jnp.sin(ang)
    sin = jnp.where(first, -sin, sin)
    pad = lambda tab, v: jnp.tile(
        jnp.concatenate([jnp.full((CTX_LEN, RET_QK_DIM), v, F32), tab], axis=0), (bsz, 1))
    return pad(cos, 1.0), pad(sin, 0.0)


def _chunk_order(step, n_ctx, n_all, reverse):
    if not reverse:
        return step
    return jnp.where(step < n_ctx, n_ctx - 1 - step, n_all - 1 + n_ctx - step)


def _tri_mask(reverse):
    i = lax.broadcasted_iota(jnp.int32, (CHUNK, CHUNK), 0)
    j = lax.broadcasted_iota(jnp.int32, (CHUNK, CHUNK), 1)
    return (j >= i) if reverse else (j <= i)


def _scan_chunks(n_rows, reverse):
    order = range(n_rows // CHUNK)
    return [slice(c * CHUNK, (c + 1) * CHUNK) for c in (reversed(order) if reverse else order)]


def _ssd_kernel(xbc_ref, dt_ref, bias_ref, alog_ref, y_ref, h_ref, *, reverse, d_ssd, heads):
    @pl.when(pl.program_id(1) == 0)
    def _():
        h_ref[...] = jnp.zeros_like(h_ref)

    for rows in _scan_chunks(y_ref.shape[1], reverse):
        _ssd_chunk(xbc_ref, dt_ref, bias_ref, alog_ref, y_ref, h_ref, rows, reverse, d_ssd, heads)


def _ssd_chunk(xbc_ref, dt_ref, bias_ref, alog_ref, y_ref, h_ref, rows, reverse, d_ssd, heads):
    hpg = heads // SSD_GROUPS
    gw = hpg * SSD_HEAD_DIM
    d = 1 if reverse else 0
    mask = _tri_mask(reverse)
    hpt = LANES // SSD_HEAD_DIM
    lane_head = lax.broadcasted_iota(jnp.int32, (CHUNK, LANES), 1) // SSD_HEAD_DIM
    lane_head_row = lax.broadcasted_iota(jnp.int32, (1, LANES), 1) // SSD_HEAD_DIM
    lo = d * heads
    dtv = _softplus(dt_ref[0, rows] + bias_ref[...])[:, lo:lo + heads]
    la = dtv * ((-1.0 / math.log(2.0)) * jnp.exp(alog_ref[...]))[:, lo:lo + heads]
    acs = jnp.dot(mask.astype(F32), la, precision=lax.Precision.HIGHEST,
                  preferred_element_type=F32)
    tot = acs[0:1, :] if reverse else acs[CHUNK - 1:CHUNK, :]
    e_tot = jnp.exp2(tot)
    w = jnp.exp2(tot - acs) * dtv
    acs_w_t = jnp.concatenate([acs, w], axis=1).T
    acs_t, w_t = acs_w_t[:heads], acs_w_t[heads:]
    dt_t = jnp.concatenate([dtv, dtv], axis=1).T[:heads]

    bm_off = d_ssd
    cm_off = d_ssd + SSD_GROUPS * SSD_STATE
    for g in range(SSD_GROUPS):
        cg = xbc_ref[0, rows, cm_off + g * SSD_STATE:cm_off + (g + 1) * SSD_STATE]
        bg = xbc_ref[0, rows, bm_off + g * SSD_STATE:bm_off + (g + 1) * SSD_STATE]
        scores = lax.dot_general(cg, bg, (((1,), (1,)), ((), ())), preferred_element_type=F32)
        cg32 = cg.astype(F32)
        bg_t = bg.astype(F32).T
        hg = h_ref[g]
        hg16 = hg.astype(BF16)
        xg = xbc_ref[0, rows, g * gw:(g + 1) * gw]
        for lt in range(gw // LANES):
            cols = slice(lt * LANES, (lt + 1) * LANES)
            xl, hl = xg[:, cols], hg16[:, cols]
            y = upd = dec = None
            for sub in range(hpt):
                hd = g * hpg + lt * hpt + sub
                col_b = jnp.broadcast_to(acs[:, hd:hd + 1], (CHUNK, CHUNK))
                dm = jnp.exp2(jnp.where(mask, col_b - acs_t[hd:hd + 1, :], NEG_BIG))
                p = (scores * dm * dt_t[hd:hd + 1, :]).astype(BF16)
                c_in = (cg32 * jnp.exp2(col_b)).astype(BF16)
                y_h = (jnp.dot(p, xl, preferred_element_type=F32)
                       + jnp.dot(c_in, hl, preferred_element_type=F32))
                b_out = (bg_t * w_t[hd:hd + 1, :]).astype(BF16)
                upd_h = jnp.dot(b_out, xl, preferred_element_type=F32)
                dec_h = jnp.broadcast_to(e_tot[:, hd:hd + 1], (1, LANES))
                mine = lane_head == sub
                y = y_h if y is None else jnp.where(mine, y_h, y)
                upd = upd_h if upd is None else jnp.where(mine, upd_h, upd)
                dec = dec_h if dec is None else jnp.where(lane_head_row == sub, dec_h, dec)
            y_ref[0, rows, g * gw + lt * LANES:g * gw + (lt + 1) * LANES] = y.astype(y_ref.dtype)
            h_ref[g, :, cols] = hg[:, cols] * dec + upd


def _ssd_scan(xbc, dt, dt_bias, a_log, reverse):
    bsz, t, _ = xbc.shape
    heads = dt.shape[2] // 2
    d_ssd = heads * SSD_HEAD_DIM
    nc, n_ctx = t // SCAN_ROWS, CTX_LEN // SCAN_ROWS
    order = lambda b, s: (b, _chunk_order(s, n_ctx, nc, reverse), 0)
    return pl.pallas_call(
        functools.partial(_ssd_kernel, reverse=reverse, d_ssd=d_ssd, heads=heads),
        grid=(bsz, nc),
        in_specs=[pl.BlockSpec((1, SCAN_ROWS, xbc.shape[2]), order),
                  pl.BlockSpec((1, SCAN_ROWS, 2 * heads), order),
                  pl.BlockSpec((1, 2 * heads), lambda b, s: (0, 0)),
                  pl.BlockSpec((1, 2 * heads), lambda b, s: (0, 0))],
        out_specs=pl.BlockSpec((1, SCAN_ROWS, d_ssd), order),
        out_shape=jax.ShapeDtypeStruct((bsz, t, d_ssd), BF16),
        scratch_shapes=[pltpu.VMEM((SSD_GROUPS, SSD_STATE, d_ssd // SSD_GROUPS), F32)],
        name="ssd_scan_bwd" if reverse else "ssd_scan_fwd",
        compiler_params=_cparams("parallel", "arbitrary"),
    )(xbc, dt, dt_bias.reshape(1, -1), a_log.reshape(1, -1))


def _ret_kernel(q_ref, k_ref, v_ref, y_ref, h_ref, dec_ref, *, reverse, v_dim):
    heads = range(RET_HEADS)
    lgs = [math.log1p(-(2.0 ** (-5.0 - (RET_HEADS - 1 - h if reverse else h)))) for h in heads]

    @pl.when(pl.program_id(1) == 0)
    def _():
        h_ref[...] = jnp.zeros_like(h_ref)
        mask = _tri_mask(reverse)
        i = lax.broadcasted_iota(jnp.int32, (CHUNK, CHUNK), 0)
        j = lax.broadcasted_iota(jnp.int32, (CHUNK, CHUNK), 1)
        dist = (jnp.abs(i - j)).astype(F32)
        n_in = ((CHUNK - i) if reverse else (i + 1)).astype(F32)
        n_out = (i if reverse else (CHUNK - 1 - i)).astype(F32)
        for h in heads:
            lg2 = lgs[h] / math.log(2.0)
            dec_ref[0, h] = jnp.where(mask, jnp.exp2(lg2 * dist), 0.0)
            dec_ref[1, h] = jnp.exp2(lg2 * n_in)
            dec_ref[2, h] = jnp.exp2(lg2 * n_out)

    for rows in _scan_chunks(y_ref.shape[1], reverse):
        _ret_chunk(q_ref, k_ref, v_ref, y_ref, h_ref, dec_ref, rows, lgs, v_dim)


def _ret_chunk(q_ref, k_ref, v_ref, y_ref, h_ref, dec_ref, rows, lgs, v_dim):
    heads = range(RET_HEADS)
    reps = v_dim // CHUNK
    qs = [q_ref[0, rows, h * RET_QK_DIM:(h + 1) * RET_QK_DIM] for h in heads]
    ks = [k_ref[0, rows, h * RET_QK_DIM:(h + 1) * RET_QK_DIM] for h in heads]
    vs = [v_ref[0, rows, h * v_dim:(h + 1) * v_dim].astype(BF16) for h in heads]
    hs = [h_ref[h] for h in heads]
    scores = [lax.dot_general(qs[h], ks[h], (((1,), (1,)), ((), ())),
                              preferred_element_type=F32) for h in heads]
    inter = [jnp.dot(qs[h], hs[h].astype(BF16), preferred_element_type=F32) for h in heads]
    ps = [(scores[h] * dec_ref[0, h]).astype(BF16) for h in heads]
    k_out = [(ks[h].astype(F32) * dec_ref[2, h]).astype(BF16) for h in heads]
    intra = [jnp.dot(ps[h], vs[h], preferred_element_type=F32) for h in heads]
    upd = [lax.dot_general(k_out[h], vs[h], (((0,), (0,)), ((), ())),
                           preferred_element_type=F32) for h in heads]
    for h in heads:
        e_in = jnp.concatenate([dec_ref[1, h]] * reps, axis=1)
        y_ref[0, rows, h * v_dim:(h + 1) * v_dim] = (intra[h] + e_in * inter[h]).astype(y_ref.dtype)
    for h in heads:
        h_ref[h] = hs[h] * math.exp(lgs[h] * CHUNK) + upd[h]


def _ret_scan(qkv, v_blk, v_width, reverse):
    qk = pr = qkv
    bsz, t, _ = qkv.shape
    qk_width = RET_HEADS * RET_QK_DIM
    v_dim = v_width // RET_HEADS
    nc, n_ctx = t // SCAN_ROWS, CTX_LEN // SCAN_ROWS
    order = lambda b, s: (b, _chunk_order(s, n_ctx, nc, reverse), 0)
    col = lambda c: (lambda b, s: (b, _chunk_order(s, n_ctx, nc, reverse), c))
    return pl.pallas_call(
        functools.partial(_ret_kernel, reverse=reverse, v_dim=v_dim),
        grid=(bsz, nc),
        in_specs=[pl.BlockSpec((1, SCAN_ROWS, qk_width), col(0)),
                  pl.BlockSpec((1, SCAN_ROWS, qk_width), col(1)),
                  pl.BlockSpec((1, SCAN_ROWS, v_width), col(v_blk))],
        out_specs=pl.BlockSpec((1, SCAN_ROWS, v_width), order),
        out_shape=jax.ShapeDtypeStruct((bsz, t, v_width), BF16),
        scratch_shapes=[pltpu.VMEM((RET_HEADS, RET_QK_DIM, v_dim), F32),
                        pltpu.VMEM((3, RET_HEADS, CHUNK, CHUNK), F32)],
        name="ret_scan_bwd" if reverse else "ret_scan_fwd",
        compiler_params=_cparams("parallel", "arbitrary"),
    )(qk, qk, pr)


def _even_out_kernel(ysf_ref, ysb_ref, yrf_ref, yrb_ref, xs_ref, z_ref, g_ref,
                     dsk_ref, sn_ref, rn_ref, o_ref, *, d_ssd, v_dim):
    ys = (ysf_ref[0].astype(F32) + ysb_ref[0].astype(F32)
          + dsk_ref[...] * xs_ref[0].astype(F32))
    ys = ys * z_ref[0]
    ys = ys * lax.rsqrt(jnp.mean(ys * ys, axis=-1, keepdims=True) + EPS) * sn_ref[...]
    o_ref[0, :, :d_ssd] = ys.astype(o_ref.dtype)
    for h in range(RET_HEADS):
        cols = slice(h * v_dim, (h + 1) * v_dim)
        yr = yrf_ref[0, :, cols].astype(F32) + yrb_ref[0, :, cols].astype(F32)
        mu = jnp.mean(yr, axis=-1, keepdims=True)
        dev = yr - mu
        var = jnp.mean(dev * dev, axis=-1, keepdims=True)
        yn = dev * lax.rsqrt(var + EPS) * rn_ref[:, cols]
        o_ref[0, :, d_ssd + h * v_dim:d_ssd + (h + 1) * v_dim] = (
            yn * g_ref[0, :, cols]).astype(o_ref.dtype)


def _even_out(ysf, ysb, yrf, yrb, xbc, pa, pb, z_blk, g_blk, d_skip_row, ssd_norm, ret_norm):
    bsz, t, d_ssd = ysf.shape
    d_ret = yrf.shape[2]
    r = ROW_TILE
    blk = lambda w, c: pl.BlockSpec((1, r, w), lambda b, i: (b, i, c))
    row = lambda w: pl.BlockSpec((1, w), lambda b, i: (0, 0))
    return pl.pallas_call(
        functools.partial(_even_out_kernel, d_ssd=d_ssd, v_dim=d_ret // RET_HEADS),
        grid=(bsz, t // r),
        in_specs=[blk(d_ssd, 0), blk(d_ssd, 0), blk(d_ret, 0), blk(d_ret, 0),
                  blk(d_ssd, 0), blk(d_ssd, z_blk), blk(d_ret, g_blk),
                  row(d_ssd), row(d_ssd), row(d_ret)],
        out_specs=blk(d_ssd + d_ret, 0),
        out_shape=jax.ShapeDtypeStruct((bsz, t, d_ssd + d_ret), BF16),
        name="even_out",
        compiler_params=_cparams("parallel", "parallel"),
    )(ysf, ysb, yrf, yrb, xbc, pa, pb, d_skip_row, ssd_norm.reshape(1, -1),
      ret_norm.reshape(1, -1))


def _lru_fwd_kernel(x_ref, p_ref, n_ref, cw_ref, cb_ref, gw_ref, gb_ref, lam_ref,
                    y_ref, xf_ref, carry_ref, pad_ref, as_ref, us_ref, *, n_tiles):
    xf = _conv_taps(x_ref[0], p_ref[0], n_ref[0], cw_ref, cb_ref, pl.program_id(2), n_tiles,
                    pad_ref)
    xf_ref[0] = xf

    def emit(cols, y):
        y_ref[0, :, cols] = y

    _lru_tile(xf, gw_ref, gb_ref, lam_ref, carry_ref, as_ref, us_ref, False, emit)


def _lru_bwd_kernel(xf_ref, gw_ref, gb_ref, lam_ref, yf_ref, gate_ref, o_ref,
                    carry_ref, as_ref, us_ref):
    def emit(cols, y):
        o_ref[0, :, cols] = ((yf_ref[0, :, cols] + y) * gate_ref[0, :, cols]).astype(o_ref.dtype)

    _lru_tile(xf_ref[0], gw_ref, gb_ref, lam_ref, carry_ref, as_ref, us_ref, True, emit)


def _lru_tile(xf, gw_ref, gb_ref, lam_ref, carry_ref, as_ref, us_ref, reverse, emit):
    @pl.when(pl.program_id(2) == 0)
    def _():
        carry_ref[...] = jnp.zeros_like(carry_ref)

    r_rows, cb = xf.shape
    xb = xf.astype(BF16)
    rate = (-LRU_C / math.log(2.0)) * _softplus(-lam_ref[0])
    body = 0 if reverse else 8
    pad = r_rows if reverse else 0
    _slab_store(as_ref, pad, jnp.ones((8, cb), F32))
    _slab_store(us_ref, pad, jnp.zeros((8, cb), F32))
    n_grp = r_rows // 8
    for kb in range(cb // RNN_BLOCK):
        cols = slice(kb * RNN_BLOCK, (kb + 1) * RNN_BLOCK)
        k0, nk = kb * RNN_BLOCK // LANES, RNN_BLOCK // LANES
        xk = xb[:, cols]
        rg = jax.nn.sigmoid(jnp.dot(xk, gw_ref[0, 0, kb], preferred_element_type=F32)
                            + gb_ref[0, 0:1, cols])
        ig = jax.nn.sigmoid(jnp.dot(xk, gw_ref[0, 1, kb], preferred_element_type=F32)
                            + gb_ref[0, 1:2, cols])
        a = jnp.exp2(rate[:, cols] * rg)
        s = 1.0 - a * a
        root = s * lax.rsqrt(jnp.maximum(s, jnp.finfo(F32).tiny))
        u = root * (ig * xf[:, cols])
        for sh in (1, 2, 4):
            _slab_store(as_ref, body, a, k0)
            _slab_store(us_ref, body, u, k0)
            src = body + sh if reverse else body - sh
            u = a * _slab_load(us_ref, src, r_rows, k0, nk) + u
            a = a * _slab_load(as_ref, src, r_rows, k0, nk)
        h = carry_ref[:, cols]
        outs = [None] * n_grp
        for v in (range(n_grp - 1, -1, -1) if reverse else range(n_grp)):
            h = a[v * 8:(v + 1) * 8] * h + u[v * 8:(v + 1) * 8]
            outs[v] = h
        last = h[0:1] if reverse else h[7:8]
        carry_ref[:, cols] = jnp.broadcast_to(last, (8, RNN_BLOCK))
        emit(cols, jnp.concatenate(outs, axis=0))


def _lru_bidir(xr, gate, conv_w, conv_b, gate_w, gate_b, lam):
    bsz, t, d_rnn = xr.shape
    cb = min(COL_TILE, d_rnn)
    r = ROW_TILE
    n_tiles = t // r
    rb = r // 8
    last = n_tiles * rb - 1
    grid = (bsz, d_rnn // cb, n_tiles)
    slab = lambda rows: pltpu.VMEM((cb // LANES, rows, LANES), F32)
    lam3 = lam.reshape(2, 1, d_rnn)

    def dir_specs(d):
        return [pl.BlockSpec((1, 2, cb // RNN_BLOCK, RNN_BLOCK, RNN_BLOCK),
                             lambda b, j, s: (d, 0, j, 0, 0)),
                pl.BlockSpec((1, 2, cb), lambda b, j, s: (d, 0, j)),
                pl.BlockSpec((1, 1, cb), lambda b, j, s: (d, 0, j))]

    fwd_tile = pl.BlockSpec((1, r, cb), lambda b, j, s: (b, s, j))
    yf, xf = pl.pallas_call(
        functools.partial(_lru_fwd_kernel, n_tiles=n_tiles),
        grid=grid,
        in_specs=[fwd_tile,
                  pl.BlockSpec((1, 8, cb), lambda b, j, s: (b, jnp.maximum(s * rb - 1, 0), j)),
                  pl.BlockSpec((1, 8, cb), lambda b, j, s: (b, jnp.minimum((s + 1) * rb, last), j)),
                  pl.BlockSpec((CONV_W, cb), lambda b, j, s: (0, j)),
                  pl.BlockSpec((1, cb), lambda b, j, s: (0, j))] + dir_specs(0),
        out_specs=[fwd_tile, fwd_tile],
        out_shape=[jax.ShapeDtypeStruct((bsz, t, d_rnn), F32)] * 2,
        scratch_shapes=[pltpu.VMEM((8, cb), F32), slab(r + 16), slab(r + 8), slab(r + 8)],
        name="lru_scan_fwd",
        compiler_params=_cparams("parallel", "parallel", "arbitrary"),
    )(xr, xr, xr, conv_w, conv_b.reshape(1, d_rnn), gate_w, gate_b, lam3)

    bwd_tile = pl.BlockSpec(
        (1, r, cb), lambda b, j, s: (b, _chunk_order(s, CTX_LEN // r, n_tiles, True), j))
    return pl.pallas_call(
        _lru_bwd_kernel,
        grid=grid,
        in_specs=[bwd_tile] + dir_specs(1) + [bwd_tile, bwd_tile],
        out_specs=bwd_tile,
        out_shape=jax.ShapeDtypeStruct((bsz, t, d_rnn), BF16),
        scratch_shapes=[pltpu.VMEM((8, cb), F32), slab(r + 8), slab(r + 8)],
        name="lru_scan_bwd",
        compiler_params=_cparams("parallel", "parallel", "arbitrary"),
    )(xf, gate_w, gate_b, lam3, yf, gate)


def _even_layer(h2, bsz, t, w_in_all, layer, conv_w, conv_b, dt_bias, a_log, d_skip, ssd_norm,
                ret_norm, rope_tabs):
    d_model = h2.shape[1]
    d_ssd = d_model
    d_ret = d_model
    gn = SSD_GROUPS * SSD_STATE
    heads = d_ssd // SSD_HEAD_DIM
    qk = RET_HEADS * RET_QK_DIM
    n_a, n_dt = 2 * d_ssd + 2 * gn, 2 * heads
    n_b = w_in_all.shape[2] - n_a - n_dt
    assert n_b == 2 * qk + 2 * d_ret
    zs, xbc_pre = _matmul_epilogue(h2, w_in_all, 512, layer, 0, [(d_ssd, "silu", 1.0)],
                                   d_ssd + 2 * gn)
    qkvg, _ = _matmul_epilogue(
        h2, w_in_all, 512, layer, n_a + n_dt,
        [(qk, "rope", 1.0), (qk, "rope", RET_QK_DIM ** -0.5), (d_ret, "cast", 1.0),
         (d_ret, "silu", 1.0)], 0, tabs=rope_tabs)
    zs = zs.reshape(bsz, t, d_ssd)
    xbc_pre = xbc_pre.reshape(bsz, t, -1)
    qkvg = qkvg.reshape(bsz, t, -1)
    dt = _matmul(h2, w_in_all, n_dt, layer=layer, col0=n_a, n=n_dt).reshape(bsz, t, n_dt)
    v_blk = 2 * qk // d_ret
    g_blk = v_blk + 1

    xbc = _conv_silu(xbc_pre, conv_w, conv_b, lambda j: j, d_ssd + 2 * gn, d_ssd + 2 * gn)

    ysf = _ssd_scan(xbc, dt, dt_bias, a_log, False)
    ysb = _ssd_scan(xbc, dt, dt_bias, a_log, True)
    yrf = _ret_scan(qkvg, v_blk, d_ret, False)
    yrb = _ret_scan(qkvg, v_blk, d_ret, True)
    d_skip_row = jnp.repeat(d_skip, SSD_HEAD_DIM).reshape(1, d_ssd)
    return _even_out(ysf, ysb, yrf, yrb, xbc, zs, qkvg, 0, g_blk, d_skip_row, ssd_norm,
                     ret_norm)


def _odd_layer(h2, bsz, t, w_in_all, layer, conv_w, conv_b, gate_w, gate_b, lam):
    d_rnn = conv_w.shape[1]
    gate, xr = _matmul_epilogue(h2, w_in_all, 512, layer, 0, [(d_rnn, "silu", 1.0)], d_rnn)
    gate = gate.reshape(bsz, t, d_rnn)
    xr = xr.reshape(bsz, t, d_rnn)
    return _lru_bidir(xr, gate, conv_w, conv_b, gate_w.astype(BF16), gate_b, lam)


def kernel(x, c, ctx, c_ctx, ada_w, ada_b, norm_pre, norm_post, e_w_in, e_conv_w, e_conv_b,
           e_dt_bias, e_a_log, e_d_skip, e_ssd_norm, e_ret_norm, e_w_out, o_w_in, o_conv_w,
           o_conv_b, o_gate_w, o_gate_b, o_lambda, o_w_out):
    bsz, seq, d = x.shape
    depth = ada_w.shape[0]
    t = CTX_LEN + seq
    m = bsz * t
    mod = _modulation(jnp.concatenate([c, c_ctx[None]], axis=0), ada_w, ada_b)
    n_ctx_blk, n_lat_blk = CTX_LEN // ROW_TILE, seq // ROW_TILE
    mod_lat = jnp.broadcast_to(mod[:, :bsz].reshape(depth, bsz, 1, 3, d),
                               (depth, bsz, n_lat_blk, 3, d))
    mod_ctx = jnp.broadcast_to(mod[:, bsz].reshape(depth, 1, 1, 3, d),
                               (depth, bsz, n_ctx_blk, 3, d))
    modblk = jnp.concatenate([mod_ctx, mod_lat], axis=2).reshape(depth, m // ROW_TILE, 3, d)

    rope_tabs = _rope_tables(seq, bsz)
    xs, h2 = _prenorm(ctx, x, norm_pre[0], modblk[0])
    xs, h2 = xs.reshape(m, d), h2.reshape(m, d)
    for i in range(depth):
        j = i // 2
        if i % 2 == 0:
            y = _even_layer(h2, bsz, t, e_w_in, j, e_conv_w[j], e_conv_b[j], e_dt_bias[j],
                            e_a_log[j], e_d_skip[j], e_ssd_norm[j], e_ret_norm[j], rope_tabs)
            w_out = e_w_out
        else:
            y = _odd_layer(h2, bsz, t, o_w_in, j, o_conv_w[j], o_conv_b[j], o_gate_w[j],
                           o_gate_b[j], o_lambda[j])
            w_out = o_w_out
        out = _matmul_ksplit(y.reshape(m, -1), w_out, j, 512)
        if i + 1 < depth:
            xs, h2 = _resnorm(xs, out, modblk[i], norm_post[i], norm_pre[i + 1], modblk[i + 1])
    return _final_residual(xs.reshape(bsz, t, d), out.reshape(bsz, t, d), modblk[depth - 1],
                           norm_post[depth - 1], seq)
```

```python
import functools
import math

import jax
import jax.numpy as jnp
from jax import lax
from jax.experimental import pallas as pl
from jax.experimental.pallas import tpu as pltpu

F32 = jnp.float32
BF16 = jnp.bfloat16

GRID_W = 64
CTX_LEN = 256
SSD_HEAD_DIM = 64
SSD_GROUPS = 8
SSD_STATE = 128
RET_HEADS = 16
RET_QK_DIM = 128
RNN_BLOCK = 256
LRU_C = 8.0
CONV_W = 4
CHUNK = 128
ROPE_BASE = 10000.0
EPS = 1e-6

LANES = 128
MXU_COLS = 256
ROW_TILE = 256
COL_TILE = 2048
SCAN_ROWS = 2 * CHUNK
PROJ_COLS = 512
MOD_COLS = 1024
ROW_PANELS = (1536, 1024, 768, 512, 256)
A_TILE_BYTES = 12 * 1024 * 1024
VMEM_LIMIT = 56 * 1024 * 1024
NEG_BIG = -1e30


def _cparams(*sem):
    return pltpu.CompilerParams(dimension_semantics=sem, vmem_limit_bytes=VMEM_LIMIT)


def _silu(x):
    return x * jax.nn.sigmoid(x)


def _softplus(x):
    return jnp.maximum(x, 0.0) + jnp.log1p(jnp.exp(-jnp.abs(x)))


def _mod_kernel(ct_ref, w_ref, b_ref, o_ref, *, n_vec):
    w = w_ref[0]
    o_ref[0] = jnp.zeros(o_ref.shape[1:], F32)
    for m in range(n_vec):
        col = _silu(ct_ref[:, m:m + 1])
        o_ref[0, m:m + 1, :] = jnp.sum(w * col, axis=0, keepdims=True) + b_ref[0]


def _modulation(cvecs, ada_w, ada_b):
    depth, d, n3 = ada_w.shape
    n_vec = cvecs.shape[0]
    ct = jnp.zeros((d, 8), F32).at[:, :n_vec].set(cvecs.T)
    tn = MOD_COLS
    return pl.pallas_call(
        functools.partial(_mod_kernel, n_vec=n_vec),
        grid=(depth, n3 // tn),
        in_specs=[pl.BlockSpec((d, 8), lambda i, j: (0, 0)),
                  pl.BlockSpec((1, d, tn), lambda i, j: (i, 0, j)),
                  pl.BlockSpec((1, 1, tn), lambda i, j: (i, 0, j))],
        out_specs=pl.BlockSpec((1, 8, tn), lambda i, j: (i, 0, j)),
        out_shape=jax.ShapeDtypeStruct((depth, 8, n3), F32),
        name="modulation",
        compiler_params=_cparams("parallel", "parallel"),
    )(ct, ada_w, ada_b.reshape(depth, 1, n3))


def _prenorm_kernel(ctx_ref, x_ref, g_ref, mod_ref, xs_ref, h_ref, *, ctx_tiles):
    x = jnp.where(pl.program_id(1) < ctx_tiles, ctx_ref[0], x_ref[0])
    xs_ref[0] = x
    y = x * lax.rsqrt(jnp.mean(x * x, axis=-1, keepdims=True) + EPS) * g_ref[...]
    h_ref[0] = (y * (1.0 + mod_ref[0, 1:2, :]) + mod_ref[0, 0:1, :]).astype(h_ref.dtype)


def _prenorm(ctx, x, g, modblk):
    bsz, seq, d = x.shape
    r = ROW_TILE
    ctx_tiles = ctx.shape[1] // r
    n_tiles = ctx_tiles + seq // r
    tile = pl.BlockSpec((1, r, d), lambda b, i: (b, i, 0))
    return pl.pallas_call(
        functools.partial(_prenorm_kernel, ctx_tiles=ctx_tiles),
        grid=(bsz, n_tiles),
        in_specs=[pl.BlockSpec((1, r, d), lambda b, i: (b, jnp.minimum(i, ctx_tiles - 1), 0)),
                  pl.BlockSpec((1, r, d), lambda b, i: (b, jnp.maximum(i - ctx_tiles, 0), 0)),
                  pl.BlockSpec((1, d), lambda b, i: (0, 0)),
                  pl.BlockSpec((1, 3, d), lambda b, i: (b * n_tiles + i, 0, 0))],
        out_specs=[tile, tile],
        out_shape=[jax.ShapeDtypeStruct((bsz, n_tiles * r, d), F32),
                   jax.ShapeDtypeStruct((bsz, n_tiles * r, d), BF16)],
        name="prenorm",
        compiler_params=_cparams("parallel", "parallel"),
    )(ctx, x, g.reshape(1, d), modblk)


def _residual(x, y, gate, g_post):
    return x + gate * (y * lax.rsqrt(jnp.mean(y * y, axis=-1, keepdims=True) + EPS) * g_post)


def _resnorm_kernel(x_ref, y_ref, modp_ref, gpost_ref, gpre_ref, modn_ref, xn_ref, h_ref):
    xn = _residual(x_ref[...], y_ref[...], modp_ref[0, 2:3, :], gpost_ref[...])
    xn_ref[...] = xn
    hn = xn * lax.rsqrt(jnp.mean(xn * xn, axis=-1, keepdims=True) + EPS) * gpre_ref[...]
    h_ref[...] = (hn * (1.0 + modn_ref[0, 1:2, :]) + modn_ref[0, 0:1, :]).astype(h_ref.dtype)


def _resnorm(x2, y2, modp, g_post, g_pre, modn):
    m, d = x2.shape
    r = ROW_TILE
    tile = pl.BlockSpec((r, d), lambda i: (i, 0))
    vec = pl.BlockSpec((1, d), lambda i: (0, 0))
    mod = pl.BlockSpec((1, 3, d), lambda i: (i, 0, 0))
    return pl.pallas_call(
        _resnorm_kernel,
        grid=(m // r,),
        in_specs=[tile, tile, mod, vec, vec, mod],
        out_specs=[tile, tile],
        out_shape=[jax.ShapeDtypeStruct((m, d), F32), jax.ShapeDtypeStruct((m, d), BF16)],
        name="resnorm",
        compiler_params=_cparams("parallel"),
    )(x2, y2, modp, g_post.reshape(1, d), g_pre.reshape(1, d), modn)


def _final_kernel(x_ref, y_ref, modp_ref, gpost_ref, o_ref):
    o_ref[0] = _residual(x_ref[0], y_ref[0], modp_ref[0, 2:3, :], gpost_ref[...])


def _final_residual(x3, y3, modp, g_post, seq):
    bsz, t, d = x3.shape
    r = ROW_TILE
    skip, per = (t - seq) // r, t // r
    tile = pl.BlockSpec((1, r, d), lambda b, i: (b, i + skip, 0))
    return pl.pallas_call(
        _final_kernel,
        grid=(bsz, seq // r),
        in_specs=[tile, tile,
                  pl.BlockSpec((1, 3, d), lambda b, i: (b * per + i + skip, 0, 0)),
                  pl.BlockSpec((1, d), lambda b, i: (0, 0))],
        out_specs=pl.BlockSpec((1, r, d), lambda b, i: (b, i, 0)),
        out_shape=jax.ShapeDtypeStruct((bsz, seq, d), F32),
        name="final_residual",
        compiler_params=_cparams("parallel", "parallel"),
    )(x3, y3, modp, g_post.reshape(1, d))


def _mm_tile(a_ref, b_ref, cols=slice(None)):
    b = b_ref[0, :, cols] if len(b_ref.shape) == 3 else b_ref[:, cols]
    return jnp.dot(a_ref[...], b.astype(BF16), preferred_element_type=F32)


def _mm_kernel(a_ref, b_ref, o_ref):
    o_ref[...] = _mm_tile(a_ref, b_ref).astype(o_ref.dtype)


def _rope_cols(x, cos, sin, first):
    q4 = RET_QK_DIM // 4
    partner = jnp.where(first, pltpu.roll(x, RET_QK_DIM - q4, 1), pltpu.roll(x, q4, 1))
    return x * cos + partner * sin


def _proj_epilogue_kernel(*refs, segs, n16, has_tabs, has_plain):
    a_ref, b_ref = refs[0], refs[1]
    pos = 2
    if has_tabs:
        cos_ref, sin_ref = refs[2], refs[3]
        pos = 4
    o16_ref = refs[pos]
    j = pl.program_id(1)
    tn = o16_ref.shape[1]

    for lo, hi, mode, scale in segs:
        @pl.when(jnp.logical_and(j >= lo, j < hi))
        def _(mode=mode, scale=scale):
            if mode == "rope":
                cos, sin = cos_ref[...], sin_ref[...]
                lane = lax.broadcasted_iota(jnp.int32, cos.shape, 1)
                first = (lane % (RET_QK_DIM // 2)) < RET_QK_DIM // 4
            for grp in range(tn // MXU_COLS):
                acc = _mm_tile(a_ref, b_ref, slice(grp * MXU_COLS, (grp + 1) * MXU_COLS))
                if mode == "rope":
                    acc = jnp.concatenate(
                        [_rope_cols(acc[:, h * RET_QK_DIM:(h + 1) * RET_QK_DIM], cos, sin, first)
                         for h in range(MXU_COLS // RET_QK_DIM)], axis=1)
                    if scale != 1.0:
                        acc = acc * scale
                elif mode == "silu":
                    acc = _silu(acc)
                o16_ref[:, grp * MXU_COLS:(grp + 1) * MXU_COLS] = acc.astype(o16_ref.dtype)

    if has_plain:
        @pl.when(j >= n16)
        def _():
            refs[pos + 1][...] = _mm_tile(a_ref, b_ref)


def _proj_operands(a, b, tn, layer, col0, n):
    m, k = a.shape
    tm = next(c for c in ROW_PANELS
              if m % c == 0 and c * k * a.dtype.itemsize <= A_TILE_BYTES)
    if layer is not None and (col0 % LANES or tn % LANES):
        b, layer = b[layer, :, col0:col0 + n].astype(BF16), None
    if layer is None:
        b_spec = pl.BlockSpec((k, tn), lambda i, j: (0, j))
    else:
        b_spec = pl.BlockSpec(
            (pl.Element(1), pl.Element(k), pl.Element(tn)),
            lambda i, j: (layer, 0, pl.multiple_of(col0 + j * tn, LANES)))
    return tm, b, b_spec


def _matmul(a, b, tn, out_dtype=F32, layer=None, col0=0, n=None):
    m, k = a.shape
    n = b.shape[-1] if n is None else n
    tm, b, b_spec = _proj_operands(a, b, tn, layer, col0, n)
    return pl.pallas_call(
        _mm_kernel,
        grid=(m // tm, n // tn),
        in_specs=[pl.BlockSpec((tm, k), lambda i, j: (i, 0)), b_spec],
        out_specs=pl.BlockSpec((tm, tn), lambda i, j: (i, j)),
        out_shape=jax.ShapeDtypeStruct((m, n), out_dtype),
        name="proj_k%d" % k,
        compiler_params=_cparams("parallel", "arbitrary"),
    )(a, b)


def _matmul_epilogue(a, b, tn, layer, col0, parts16, n_plain, tabs=None):
    m, k = a.shape
    n16 = sum(p[0] for p in parts16)
    tm, b, b_spec = _proj_operands(a, b, tn, layer, col0, n16 + n_plain)
    segs, lo = [], 0
    for n_cols, mode, scale in parts16:
        segs.append((lo, lo + n_cols // tn, mode, scale))
        lo += n_cols // tn
    t16 = lo
    in_specs = [pl.BlockSpec((tm, k), lambda i, j: (i, 0)), b_spec]
    args = [a, b]
    if tabs is not None:
        in_specs += [pl.BlockSpec((tm, RET_QK_DIM), lambda i, j: (i, 0))] * 2
        args += list(tabs)
    out_specs = [pl.BlockSpec((tm, tn), lambda i, j: (i, jnp.minimum(j, t16 - 1)))]
    out_shape = [jax.ShapeDtypeStruct((m, n16), BF16)]
    if n_plain:
        out_specs.append(pl.BlockSpec((tm, tn), lambda i, j: (i, jnp.maximum(j - t16, 0))))
        out_shape.append(jax.ShapeDtypeStruct((m, n_plain), F32))
    outs = pl.pallas_call(
        functools.partial(_proj_epilogue_kernel, segs=tuple(segs), n16=t16,
                          has_tabs=tabs is not None, has_plain=bool(n_plain)),
        grid=(m // tm, (n16 + n_plain) // tn),
        in_specs=in_specs,
        out_specs=out_specs,
        out_shape=out_shape,
        name="proj_epi_k%d" % k,
        compiler_params=_cparams("parallel", "arbitrary"),
    )(*args)
    return outs[0], (outs[1] if n_plain else None)


def _mm_acc_kernel(a_ref, b_ref, p_ref, o_ref):
    o_ref[...] = p_ref[...] + _mm_tile(a_ref, b_ref)


def _matmul_ksplit(a, b_all, layer, tn):
    m, k = a.shape
    n = b_all.shape[2]
    kh = k // 2
    tm = next(c for c in ROW_PANELS
              if m % c == 0 and c * kh * a.dtype.itemsize <= A_TILE_BYTES)
    out = None
    for half in range(2):
        a_spec = pl.BlockSpec((tm, kh), lambda i, j, half=half: (i, half))
        b_spec = pl.BlockSpec(
            (pl.Element(1), pl.Element(kh), pl.Element(tn)),
            lambda i, j, half=half: (layer, half * kh, pl.multiple_of(j * tn, LANES)))
        o_spec = pl.BlockSpec((tm, tn), lambda i, j: (i, j))
        first = out is None
        out = pl.pallas_call(
            _mm_kernel if first else _mm_acc_kernel,
            grid=(m // tm, n // tn),
            in_specs=[a_spec, b_spec] + ([] if first else [o_spec]),
            out_specs=o_spec,
            out_shape=jax.ShapeDtypeStruct((m, n), F32),
            name="proj_half_k%d" % k,
            compiler_params=_cparams("parallel", "arbitrary"),
        )(*([a, b_all] if first else [a, b_all, out]))
    return out


def _slab_store(ref, row0, val, k0=0):
    for k in range(val.shape[1] // LANES):
        ref[k0 + k, row0:row0 + val.shape[0], :] = val[:, k * LANES:(k + 1) * LANES]


def _slab_load(ref, row0, rows, k0=0, n=None):
    n = ref.shape[0] if n is None else n
    return jnp.concatenate([ref[k0 + k, pl.ds(row0, rows, stride=1), :] for k in range(n)],
                           axis=1)


def _conv_taps(x, prev8, next8, w_ref, b_ref, tile, n_tiles, pad_ref):
    r = x.shape[0]
    ctx_tiles = CTX_LEN // r
    prev_ok = jnp.logical_and(tile != 0, tile != ctx_tiles)
    next_ok = jnp.logical_and(tile != ctx_tiles - 1, tile != n_tiles - 1)
    _slab_store(pad_ref, 0, jnp.where(prev_ok, prev8, 0.0))
    _slab_store(pad_ref, 8, x)
    _slab_store(pad_ref, 8 + r, jnp.where(next_ok, next8, 0.0))
    return (b_ref[...] + w_ref[0:1, :] * _slab_load(pad_ref, 6, r)
            + w_ref[1:2, :] * _slab_load(pad_ref, 7, r)
            + w_ref[2:3, :] * x + w_ref[3:4, :] * _slab_load(pad_ref, 9, r))


def _conv_silu_kernel(x_ref, p_ref, n_ref, w_ref, b_ref, o_ref, pad_ref, *, n_tiles):
    y = _conv_taps(x_ref[0], p_ref[0], n_ref[0], w_ref, b_ref, pl.program_id(1), n_tiles, pad_ref)
    o_ref[0] = _silu(y).astype(o_ref.dtype)


def _conv_silu(x3, conv_w, conv_b):
    bsz, t, c = x3.shape
    r = ROW_TILE
    n_tiles = t // r
    rb = r // 8
    last = n_tiles * rb - 1
    return pl.pallas_call(
        functools.partial(_conv_silu_kernel, n_tiles=n_tiles),
        grid=(bsz, n_tiles),
        in_specs=[pl.BlockSpec((1, r, c), lambda b, i: (b, i, 0)),
                  pl.BlockSpec((1, 8, c), lambda b, i: (b, jnp.maximum(i * rb - 1, 0), 0)),
                  pl.BlockSpec((1, 8, c), lambda b, i: (b, jnp.minimum((i + 1) * rb, last), 0)),
                  pl.BlockSpec((CONV_W, c), lambda b, i: (0, 0)),
                  pl.BlockSpec((1, c), lambda b, i: (0, 0))],
        out_specs=pl.BlockSpec((1, r, c), lambda b, i: (b, i, 0)),
        out_shape=jax.ShapeDtypeStruct((bsz, t, c), BF16),
        scratch_shapes=[pltpu.VMEM((c // LANES, r + 16, LANES), F32)],
        name="conv_silu",
        compiler_params=_cparams("parallel", "parallel"),
    )(x3, x3, x3, conv_w, conv_b.reshape(1, c))


def _rope_tables(seq, bsz):
    rows = seq // GRID_W
    row = jnp.repeat(jnp.arange(rows), GRID_W).astype(F32)
    col = jnp.tile(jnp.arange(GRID_W), rows).astype(F32)
    q4 = RET_QK_DIM // 4
    freq = ROPE_BASE ** (-jnp.arange(q4, dtype=F32) / q4)
    ang = jnp.concatenate([row[:, None] * freq, row[:, None] * freq,
                           col[:, None] * freq, col[:, None] * freq], axis=1)
    first = (jnp.arange(RET_QK_DIM) % (2 * q4)) < q4
    cos, sin = jnp.cos(ang), jnp.sin(ang)
    sin = jnp.where(first, -sin, sin)
    pad = lambda tab, v: jnp.tile(
        jnp.concatenate([jnp.full((CTX_LEN, RET_QK_DIM), v, F32), tab], axis=0), (bsz, 1))
    return pad(cos, 1.0), pad(sin, 0.0)


def _chunk_order(step, n_ctx, n_all, reverse):
    if not reverse:
        return step
    return jnp.where(step < n_ctx, n_ctx - 1 - step, n_all - 1 + n_ctx - step)


def _tri_mask(reverse):
    i = lax.broadcasted_iota(jnp.int32, (CHUNK, CHUNK), 0)
    j = lax.broadcasted_iota(jnp.int32, (CHUNK, CHUNK), 1)
    return (j >= i) if reverse else (j <= i)


def _scan_chunks(n_rows, reverse):
    order = range(n_rows // CHUNK)
    return [slice(c * CHUNK, (c + 1) * CHUNK) for c in (reversed(order) if reverse else order)]


def _ssd_kernel(xbc_ref, dt_ref, bias_ref, alog_ref, y_ref, h_ref, *, reverse, d_ssd, heads):
    @pl.when(pl.program_id(1) == 0)
    def _():
        h_ref[...] = jnp.zeros_like(h_ref)

    for rows in _scan_chunks(y_ref.shape[1], reverse):
        _ssd_chunk(xbc_ref, dt_ref, bias_ref, alog_ref, y_ref, h_ref, rows, reverse, d_ssd, heads)


def _ssd_chunk(xbc_ref, dt_ref, bias_ref, alog_ref, y_ref, h_ref, rows, reverse, d_ssd, heads):
    hpg = heads // SSD_GROUPS
    gw = hpg * SSD_HEAD_DIM
    d = 1 if reverse else 0
    mask = _tri_mask(reverse)
    hpt = LANES // SSD_HEAD_DIM
    lane_head = lax.broadcasted_iota(jnp.int32, (CHUNK, LANES), 1) // SSD_HEAD_DIM
    lane_head_row = lax.broadcasted_iota(jnp.int32, (1, LANES), 1) // SSD_HEAD_DIM
    lo = d * heads
    dtv = _softplus(dt_ref[0, rows] + bias_ref[...])[:, lo:lo + heads]
    la = dtv * ((-1.0 / math.log(2.0)) * jnp.exp(alog_ref[...]))[:, lo:lo + heads]
    acs = jnp.dot(mask.astype(F32), la, precision=lax.Precision.HIGHEST,
                  preferred_element_type=F32)
    tot = acs[0:1, :] if reverse else acs[CHUNK - 1:CHUNK, :]
    e_tot = jnp.exp2(tot)
    w = jnp.exp2(tot - acs) * dtv
    acs_w_t = jnp.concatenate([acs, w], axis=1).T
    acs_t, w_t = acs_w_t[:heads], acs_w_t[heads:]
    dt_t = jnp.concatenate([dtv, dtv], axis=1).T[:heads]

    bm_off = d_ssd
    cm_off = d_ssd + SSD_GROUPS * SSD_STATE
    for g in range(SSD_GROUPS):
        cg = xbc_ref[0, rows, cm_off + g * SSD_STATE:cm_off + (g + 1) * SSD_STATE]
        bg = xbc_ref[0, rows, bm_off + g * SSD_STATE:bm_off + (g + 1) * SSD_STATE]
        scores = lax.dot_general(cg, bg, (((1,), (1,)), ((), ())), preferred_element_type=F32)
        cg32 = cg.astype(F32)
        bg_t = bg.astype(F32).T
        hg = h_ref[g]
        hg16 = hg.astype(BF16)
        xg = xbc_ref[0, rows, g * gw:(g + 1) * gw]
        for lt in range(gw // LANES):
            cols = slice(lt * LANES, (lt + 1) * LANES)
            xl, hl = xg[:, cols], hg16[:, cols]
            y = upd = dec = None
            for sub in range(hpt):
                hd = g * hpg + lt * hpt + sub
                col_b = jnp.broadcast_to(acs[:, hd:hd + 1], (CHUNK, CHUNK))
                dm = jnp.exp2(jnp.where(mask, col_b - acs_t[hd:hd + 1, :], NEG_BIG))
                p = (scores * dm * dt_t[hd:hd + 1, :]).astype(BF16)
                c_in = (cg32 * jnp.exp2(col_b)).astype(BF16)
                y_h = (jnp.dot(p, xl, preferred_element_type=F32)
                       + jnp.dot(c_in, hl, preferred_element_type=F32))
                b_out = (bg_t * w_t[hd:hd + 1, :]).astype(BF16)
                upd_h = jnp.dot(b_out, xl, preferred_element_type=F32)
                dec_h = jnp.broadcast_to(e_tot[:, hd:hd + 1], (1, LANES))
                mine = lane_head == sub
                y = y_h if y is None else jnp.where(mine, y_h, y)
                upd = upd_h if upd is None else jnp.where(mine, upd_h, upd)
                dec = dec_h if dec is None else jnp.where(lane_head_row == sub, dec_h, dec)
            y_ref[0, rows, g * gw + lt * LANES:g * gw + (lt + 1) * LANES] = y.astype(y_ref.dtype)
            h_ref[g, :, cols] = hg[:, cols] * dec + upd


def _ssd_scan(xbc, dt, dt_bias, a_log, reverse):
    bsz, t, _ = xbc.shape
    heads = dt.shape[2] // 2
    d_ssd = heads * SSD_HEAD_DIM
    nc, n_ctx = t // SCAN_ROWS, CTX_LEN // SCAN_ROWS
    order = lambda b, s: (b, _chunk_order(s, n_ctx, nc, reverse), 0)
    return pl.pallas_call(
        functools.partial(_ssd_kernel, reverse=reverse, d_ssd=d_ssd, heads=heads),
        grid=(bsz, nc),
        in_specs=[pl.BlockSpec((1, SCAN_ROWS, xbc.shape[2]), order),
                  pl.BlockSpec((1, SCAN_ROWS, 2 * heads), order),
                  pl.BlockSpec((1, 2 * heads), lambda b, s: (0, 0)),
                  pl.BlockSpec((1, 2 * heads), lambda b, s: (0, 0))],
        out_specs=pl.BlockSpec((1, SCAN_ROWS, d_ssd), order),
        out_shape=jax.ShapeDtypeStruct((bsz, t, d_ssd), BF16),
        scratch_shapes=[pltpu.VMEM((SSD_GROUPS, SSD_STATE, d_ssd // SSD_GROUPS), F32)],
        name="ssd_scan_bwd" if reverse else "ssd_scan_fwd",
        compiler_params=_cparams("parallel", "arbitrary"),
    )(xbc, dt, dt_bias.reshape(1, -1), a_log.reshape(1, -1))


def _ret_kernel(q_ref, k_ref, v_ref, y_ref, h_ref, dec_ref, *, reverse, v_dim):
    heads = range(RET_HEADS)
    lgs = [math.log1p(-(2.0 ** (-5.0 - (RET_HEADS - 1 - h if reverse else h)))) for h in heads]

    @pl.when(pl.program_id(1) == 0)
    def _():
        h_ref[...] = jnp.zeros_like(h_ref)
        mask = _tri_mask(reverse)
        i = lax.broadcasted_iota(jnp.int32, (CHUNK, CHUNK), 0)
        j = lax.broadcasted_iota(jnp.int32, (CHUNK, CHUNK), 1)
        dist = (jnp.abs(i - j)).astype(F32)
        n_in = ((CHUNK - i) if reverse else (i + 1)).astype(F32)
        n_out = (i if reverse else (CHUNK - 1 - i)).astype(F32)
        for h in heads:
            lg2 = lgs[h] / math.log(2.0)
            dec_ref[0, h] = jnp.where(mask, jnp.exp2(lg2 * dist), 0.0)
            dec_ref[1, h] = jnp.exp2(lg2 * n_in)
            dec_ref[2, h] = jnp.exp2(lg2 * n_out)

    for rows in _scan_chunks(y_ref.shape[1], reverse):
        _ret_chunk(q_ref, k_ref, v_ref, y_ref, h_ref, dec_ref, rows, lgs, v_dim)


def _ret_chunk(q_ref, k_ref, v_ref, y_ref, h_ref, dec_ref, rows, lgs, v_dim):
    heads = range(RET_HEADS)
    reps = v_dim // CHUNK
    qs = [q_ref[0, rows, h * RET_QK_DIM:(h + 1) * RET_QK_DIM] for h in heads]
    ks = [k_ref[0, rows, h * RET_QK_DIM:(h + 1) * RET_QK_DIM] for h in heads]
    vs = [v_ref[0, rows, h * v_dim:(h + 1) * v_dim].astype(BF16) for h in heads]
    hs = [h_ref[h] for h in heads]
    scores = [lax.dot_general(qs[h], ks[h], (((1,), (1,)), ((), ())),
                              preferred_element_type=F32) for h in heads]
    inter = [jnp.dot(qs[h], hs[h].astype(BF16), preferred_element_type=F32) for h in heads]
    ps = [(scores[h] * dec_ref[0, h]).astype(BF16) for h in heads]
    k_out = [(ks[h].astype(F32) * dec_ref[2, h]).astype(BF16) for h in heads]
    intra = [jnp.dot(ps[h], vs[h], preferred_element_type=F32) for h in heads]
    upd = [lax.dot_general(k_out[h], vs[h], (((0,), (0,)), ((), ())),
                           preferred_element_type=F32) for h in heads]
    for h in heads:
        e_in = jnp.concatenate([dec_ref[1, h]] * reps, axis=1)
        y_ref[0, rows, h * v_dim:(h + 1) * v_dim] = (intra[h] + e_in * inter[h]).astype(y_ref.dtype)
    for h in heads:
        h_ref[h] = hs[h] * math.exp(lgs[h] * CHUNK) + upd[h]


def _ret_scan(qkv, v_blk, v_width, reverse):
    bsz, t, _ = qkv.shape
    qk_width = RET_HEADS * RET_QK_DIM
    v_dim = v_width // RET_HEADS
    nc, n_ctx = t // SCAN_ROWS, CTX_LEN // SCAN_ROWS
    col = lambda c: (lambda b, s: (b, _chunk_order(s, n_ctx, nc, reverse), c))
    order = col(0)
    return pl.pallas_call(
        functools.partial(_ret_kernel, reverse=reverse, v_dim=v_dim),
        grid=(bsz, nc),
        in_specs=[pl.BlockSpec((1, SCAN_ROWS, qk_width), col(0)),
                  pl.BlockSpec((1, SCAN_ROWS, qk_width), col(1)),
                  pl.BlockSpec((1, SCAN_ROWS, v_width), col(v_blk))],
        out_specs=pl.BlockSpec((1, SCAN_ROWS, v_width), order),
        out_shape=jax.ShapeDtypeStruct((bsz, t, v_width), BF16),
        scratch_shapes=[pltpu.VMEM((RET_HEADS, RET_QK_DIM, v_dim), F32),
                        pltpu.VMEM((3, RET_HEADS, CHUNK, CHUNK), F32)],
        name="ret_scan_bwd" if reverse else "ret_scan_fwd",
        compiler_params=_cparams("parallel", "arbitrary"),
    )(qkv, qkv, qkv)


def _even_out_kernel(ysf_ref, ysb_ref, yrf_ref, yrb_ref, xs_ref, z_ref, g_ref,
                     dsk_ref, sn_ref, rn_ref, o_ref, *, d_ssd, v_dim):
    ys = (ysf_ref[0].astype(F32) + ysb_ref[0].astype(F32)
          + dsk_ref[...] * xs_ref[0].astype(F32))
    ys = ys * z_ref[0]
    ys = ys * lax.rsqrt(jnp.mean(ys * ys, axis=-1, keepdims=True) + EPS) * sn_ref[...]
    o_ref[0, :, :d_ssd] = ys.astype(o_ref.dtype)
    for h in range(RET_HEADS):
        cols = slice(h * v_dim, (h + 1) * v_dim)
        yr = yrf_ref[0, :, cols].astype(F32) + yrb_ref[0, :, cols].astype(F32)
        mu = jnp.mean(yr, axis=-1, keepdims=True)
        dev = yr - mu
        var = jnp.mean(dev * dev, axis=-1, keepdims=True)
        yn = dev * lax.rsqrt(var + EPS) * rn_ref[:, cols]
        o_ref[0, :, d_ssd + h * v_dim:d_ssd + (h + 1) * v_dim] = (
            yn * g_ref[0, :, cols]).astype(o_ref.dtype)


def _even_out(ysf, ysb, yrf, yrb, xbc, pa, pb, z_blk, g_blk, d_skip_row, ssd_norm, ret_norm):
    bsz, t, d_ssd = ysf.shape
    d_ret = yrf.shape[2]
    r = ROW_TILE
    blk = lambda w, c: pl.BlockSpec((1, r, w), lambda b, i: (b, i, c))
    row = lambda w: pl.BlockSpec((1, w), lambda b, i: (0, 0))
    return pl.pallas_call(
        functools.partial(_even_out_kernel, d_ssd=d_ssd, v_dim=d_ret // RET_HEADS),
        grid=(bsz, t // r),
        in_specs=[blk(d_ssd, 0), blk(d_ssd, 0), blk(d_ret, 0), blk(d_ret, 0),
                  blk(d_ssd, 0), blk(d_ssd, z_blk), blk(d_ret, g_blk),
                  row(d_ssd), row(d_ssd), row(d_ret)],
        out_specs=blk(d_ssd + d_ret, 0),
        out_shape=jax.ShapeDtypeStruct((bsz, t, d_ssd + d_ret), BF16),
        name="even_out",
        compiler_params=_cparams("parallel", "parallel"),
    )(ysf, ysb, yrf, yrb, xbc, pa, pb, d_skip_row, ssd_norm.reshape(1, -1),
      ret_norm.reshape(1, -1))


def _lru_fwd_kernel(x_ref, p_ref, n_ref, cw_ref, cb_ref, gw_ref, gb_ref, lam_ref,
                    y_ref, xf_ref, carry_ref, pad_ref, as_ref, us_ref, *, n_tiles):
    xf = _conv_taps(x_ref[0], p_ref[0], n_ref[0], cw_ref, cb_ref, pl.program_id(2), n_tiles,
                    pad_ref)
    xf_ref[0] = xf

    def emit(cols, y):
        y_ref[0, :, cols] = y

    _lru_tile(xf, gw_ref, gb_ref, lam_ref, carry_ref, as_ref, us_ref, False, emit)


def _lru_bwd_kernel(xf_ref, gw_ref, gb_ref, lam_ref, yf_ref, gate_ref, o_ref,
                    carry_ref, as_ref, us_ref):
    def emit(cols, y):
        o_ref[0, :, cols] = ((yf_ref[0, :, cols] + y) * gate_ref[0, :, cols]).astype(o_ref.dtype)

    _lru_tile(xf_ref[0], gw_ref, gb_ref, lam_ref, carry_ref, as_ref, us_ref, True, emit)


def _lru_tile(xf, gw_ref, gb_ref, lam_ref, carry_ref, as_ref, us_ref, reverse, emit):
    @pl.when(pl.program_id(2) == 0)
    def _():
        carry_ref[...] = jnp.zeros_like(carry_ref)

    r_rows, cb = xf.shape
    xb = xf.astype(BF16)
    rate = (-LRU_C / math.log(2.0)) * _softplus(-lam_ref[0])
    body = 0 if reverse else 8
    pad = r_rows if reverse else 0
    _slab_store(as_ref, pad, jnp.ones((8, cb), F32))
    _slab_store(us_ref, pad, jnp.zeros((8, cb), F32))
    n_grp = r_rows // 8
    for kb in range(cb // RNN_BLOCK):
        cols = slice(kb * RNN_BLOCK, (kb + 1) * RNN_BLOCK)
        k0, nk = kb * RNN_BLOCK // LANES, RNN_BLOCK // LANES
        xk = xb[:, cols]
        rg = jax.nn.sigmoid(jnp.dot(xk, gw_ref[0, 0, kb], preferred_element_type=F32)
                            + gb_ref[0, 0:1, cols])
        ig = jax.nn.sigmoid(jnp.dot(xk, gw_ref[0, 1, kb], preferred_element_type=F32)
                            + gb_ref[0, 1:2, cols])
        a = jnp.exp2(rate[:, cols] * rg)
        s = 1.0 - a * a
        root = s * lax.rsqrt(jnp.maximum(s, jnp.finfo(F32).tiny))
        u = root * (ig * xf[:, cols])
        for sh in (1, 2, 4):
            _slab_store(as_ref, body, a, k0)
            _slab_store(us_ref, body, u, k0)
            src = body + sh if reverse else body - sh
            u = a * _slab_load(us_ref, src, r_rows, k0, nk) + u
            a = a * _slab_load(as_ref, src, r_rows, k0, nk)
        h = carry_ref[:, cols]
        outs = [None] * n_grp
        for v in (range(n_grp - 1, -1, -1) if reverse else range(n_grp)):
            h = a[v * 8:(v + 1) * 8] * h + u[v * 8:(v + 1) * 8]
            outs[v] = h
        last = h[0:1] if reverse else h[7:8]
        carry_ref[:, cols] = jnp.broadcast_to(last, (8, RNN_BLOCK))
        emit(cols, jnp.concatenate(outs, axis=0))


def _lru_bidir(xr, gate, conv_w, conv_b, gate_w, gate_b, lam):
    bsz, t, d_rnn = xr.shape
    cb = min(COL_TILE, d_rnn)
    r = ROW_TILE
    n_tiles = t // r
    rb = r // 8
    last = n_tiles * rb - 1
    grid = (bsz, d_rnn // cb, n_tiles)
    slab = lambda rows: pltpu.VMEM((cb // LANES, rows, LANES), F32)
    lam3 = lam.reshape(2, 1, d_rnn)

    def dir_specs(d):
        return [pl.BlockSpec((1, 2, cb // RNN_BLOCK, RNN_BLOCK, RNN_BLOCK),
                             lambda b, j, s: (d, 0, j, 0, 0)),
                pl.BlockSpec((1, 2, cb), lambda b, j, s: (d, 0, j)),
                pl.BlockSpec((1, 1, cb), lambda b, j, s: (d, 0, j))]

    fwd_tile = pl.BlockSpec((1, r, cb), lambda b, j, s: (b, s, j))
    yf, xf = pl.pallas_call(
        functools.partial(_lru_fwd_kernel, n_tiles=n_tiles),
        grid=grid,
        in_specs=[fwd_tile,
                  pl.BlockSpec((1, 8, cb), lambda b, j, s: (b, jnp.maximum(s * rb - 1, 0), j)),
                  pl.BlockSpec((1, 8, cb), lambda b, j, s: (b, jnp.minimum((s + 1) * rb, last), j)),
                  pl.BlockSpec((CONV_W, cb), lambda b, j, s: (0, j)),
                  pl.BlockSpec((1, cb), lambda b, j, s: (0, j))] + dir_specs(0),
        out_specs=[fwd_tile, fwd_tile],
        out_shape=[jax.ShapeDtypeStruct((bsz, t, d_rnn), F32)] * 2,
        scratch_shapes=[pltpu.VMEM((8, cb), F32), slab(r + 16), slab(r + 8), slab(r + 8)],
        name="lru_scan_fwd",
        compiler_params=_cparams("parallel", "parallel", "arbitrary"),
    )(xr, xr, xr, conv_w, conv_b.reshape(1, d_rnn), gate_w, gate_b, lam3)

    bwd_tile = pl.BlockSpec(
        (1, r, cb), lambda b, j, s: (b, _chunk_order(s, CTX_LEN // r, n_tiles, True), j))
    return pl.pallas_call(
        _lru_bwd_kernel,
        grid=grid,
        in_specs=[bwd_tile] + dir_specs(1) + [bwd_tile, bwd_tile],
        out_specs=bwd_tile,
        out_shape=jax.ShapeDtypeStruct((bsz, t, d_rnn), BF16),
        scratch_shapes=[pltpu.VMEM((8, cb), F32), slab(r + 8), slab(r + 8)],
        name="lru_scan_bwd",
        compiler_params=_cparams("parallel", "parallel", "arbitrary"),
    )(xf, gate_w, gate_b, lam3, yf, gate)


def _even_layer(h2, bsz, t, w_in_all, layer, conv_w, conv_b, dt_bias, a_log, d_skip, ssd_norm,
                ret_norm, rope_tabs):
    d_model = h2.shape[1]
    d_ssd = d_model
    d_ret = d_model
    gn = SSD_GROUPS * SSD_STATE
    heads = d_ssd // SSD_HEAD_DIM
    qk = RET_HEADS * RET_QK_DIM
    n_a, n_dt = 2 * d_ssd + 2 * gn, 2 * heads
    n_b = w_in_all.shape[2] - n_a - n_dt
    assert n_b == 2 * qk + 2 * d_ret
    zs, xbc_pre = _matmul_epilogue(h2, w_in_all, PROJ_COLS, layer, 0, [(d_ssd, "silu", 1.0)],
                                   d_ssd + 2 * gn)
    qkvg, _ = _matmul_epilogue(
        h2, w_in_all, PROJ_COLS, layer, n_a + n_dt,
        [(qk, "rope", 1.0), (qk, "rope", RET_QK_DIM ** -0.5), (d_ret, "cast", 1.0),
         (d_ret, "silu", 1.0)], 0, tabs=rope_tabs)
    zs = zs.reshape(bsz, t, d_ssd)
    xbc_pre = xbc_pre.reshape(bsz, t, -1)
    qkvg = qkvg.reshape(bsz, t, -1)
    dt = _matmul(h2, w_in_all, n_dt, layer=layer, col0=n_a, n=n_dt).reshape(bsz, t, n_dt)
    v_blk = 2 * qk // d_ret
    g_blk = v_blk + 1

    xbc = _conv_silu(xbc_pre, conv_w, conv_b)

    ysf = _ssd_scan(xbc, dt, dt_bias, a_log, False)
    ysb = _ssd_scan(xbc, dt, dt_bias, a_log, True)
    yrf = _ret_scan(qkvg, v_blk, d_ret, False)
    yrb = _ret_scan(qkvg, v_blk, d_ret, True)
    d_skip_row = jnp.repeat(d_skip, SSD_HEAD_DIM).reshape(1, d_ssd)
    return _even_out(ysf, ysb, yrf, yrb, xbc, zs, qkvg, 0, g_blk, d_skip_row, ssd_norm,
                     ret_norm)


def _odd_layer(h2, bsz, t, w_in_all, layer, conv_w, conv_b, gate_w, gate_b, lam):
    d_rnn = conv_w.shape[1]
    gate, xr = _matmul_epilogue(h2, w_in_all, PROJ_COLS, layer, 0, [(d_rnn, "silu", 1.0)], d_rnn)
    gate = gate.reshape(bsz, t, d_rnn)
    xr = xr.reshape(bsz, t, d_rnn)
    return _lru_bidir(xr, gate, conv_w, conv_b, gate_w.astype(BF16), gate_b, lam)


def kernel(x, c, ctx, c_ctx, ada_w, ada_b, norm_pre, norm_post, e_w_in, e_conv_w, e_conv_b,
           e_dt_bias, e_a_log, e_d_skip, e_ssd_norm, e_ret_norm, e_w_out, o_w_in, o_conv_w,
           o_conv_b, o_gate_w, o_gate_b, o_lambda, o_w_out):
    bsz, seq, d = x.shape
    depth = ada_w.shape[0]
    t = CTX_LEN + seq
    m = bsz * t
    mod = _modulation(jnp.concatenate([c, c_ctx[None]], axis=0), ada_w, ada_b)
    n_ctx_blk, n_lat_blk = CTX_LEN // ROW_TILE, seq // ROW_TILE
    mod_lat = jnp.broadcast_to(mod[:, :bsz].reshape(depth, bsz, 1, 3, d),
                               (depth, bsz, n_lat_blk, 3, d))
    mod_ctx = jnp.broadcast_to(mod[:, bsz].reshape(depth, 1, 1, 3, d),
                               (depth, bsz, n_ctx_blk, 3, d))
    modblk = jnp.concatenate([mod_ctx, mod_lat], axis=2).reshape(depth, m // ROW_TILE, 3, d)

    rope_tabs = _rope_tables(seq, bsz)
    xs, h2 = _prenorm(ctx, x, norm_pre[0], modblk[0])
    xs, h2 = xs.reshape(m, d), h2.reshape(m, d)
    for i in range(depth):
        j = i // 2
        if i % 2 == 0:
            y = _even_layer(h2, bsz, t, e_w_in, j, e_conv_w[j], e_conv_b[j], e_dt_bias[j],
                            e_a_log[j], e_d_skip[j], e_ssd_norm[j], e_ret_norm[j], rope_tabs)
            w_out = e_w_out
        else:
            y = _odd_layer(h2, bsz, t, o_w_in, j, o_conv_w[j], o_conv_b[j], o_gate_w[j],
                           o_gate_b[j], o_lambda[j])
            w_out = o_w_out
        out = _matmul_ksplit(y.reshape(m, -1), w_out, j, PROJ_COLS)
        if i + 1 < depth:
            xs, h2 = _resnorm(xs, out, modblk[i], norm_post[i], norm_pre[i + 1], modblk[i + 1])
    return _final_residual(xs.reshape(bsz, t, d), out.reshape(bsz, t, d), modblk[depth - 1],
                           norm_post[depth - 1], seq)
```
